```python
import math
import jax, jax.numpy as jnp
from jax import lax
import numpy as np

D_MODEL = 1024
BATCH = 8
SEQ = 4096
DEPTH = 4

N_MIXERS = 2
N_RET_LAYERS = (DEPTH + N_MIXERS - 1) // N_MIXERS
N_NA_LAYERS = DEPTH // N_MIXERS

MIX_WIDTH = D_MODEL
MEM_LEN = 256
MEM_HEADS = 4
MEM_WIDTH = D_MODEL // 4
MEM_HEAD_DIM = MEM_WIDTH // MEM_HEADS
TOK_WIDTH = MIX_WIDTH - MEM_WIDTH

RET_HEADS = 6
RET_HEAD_DIM = TOK_WIDTH // RET_HEADS
RET_CHUNK = 128
RET_ROPE_BASE = 10000.0
GN_EPS = 1e-5

NA_HEADS = 12
NA_HEAD_DIM = TOK_WIDTH // NA_HEADS
GRID_W = 64
NA_WIN_R = 8
NA_WIN_C = 16
NA_QB = 16
NA_KB = 32

FFN_HIDDEN = ((8 * D_MODEL + 3 * 256 - 1) // (3 * 256)) * 256
EPS = 1e-6

kernel_name = "hybrid_retention_natten_memory_encoder"


def rms_norm(x, g):
    xf = x.astype(jnp.float32)
    y = xf * lax.rsqrt(jnp.mean(xf * xf, axis=-1, keepdims=True) + EPS)
    return (y * g.astype(jnp.float32)).astype(x.dtype)


def to_heads(t, n_heads):
    b, s, w = t.shape
    return t.reshape(b, s, n_heads, w // n_heads).transpose(0, 2, 1, 3)


def from_heads(t):
    b, n, s, d = t.shape
    return t.transpose(0, 2, 1, 3).reshape(b, s, n * d)


def rotary(t, pos):
    d = t.shape[-1]
    half = d // 2
    inv = RET_ROPE_BASE ** (-jnp.arange(half, dtype=jnp.float32) / half)
    ang = pos[:, None] * inv[None, :]
    c, s = jnp.cos(ang), jnp.sin(ang)
    t1, t2 = t[..., :half], t[..., half:]
    return jnp.concatenate([t1 * c - t2 * s, t1 * s + t2 * c], axis=-1)


def retention_one_direction(q, k, v, log_gamma):
    b, h, t, dk = q.shape
    dv = v.shape[-1]
    c = RET_CHUNK
    n = t // c
    qc = q.reshape(b, h, n, c, dk)
    kc = k.reshape(b, h, n, c, dk)
    vc = v.reshape(b, h, n, c, dv)
    idx = jnp.arange(c, dtype=jnp.float32)
    rel = idx[:, None] - idx[None, :]
    lg = log_gamma[:, None, None]
    intra_decay = jnp.where(rel >= 0, jnp.exp(lg * jnp.maximum(rel, 0.0)), 0.0)
    scores = jnp.einsum('bhncd,bhnsd->bhncs', qc, kc) * intra_decay[None, :, None]
    out = jnp.einsum('bhncs,bhnse->bhnce', scores, vc)
    k_dec = jnp.exp(log_gamma[:, None] * (c - 1 - idx)[None, :])
    q_dec = jnp.exp(log_gamma[:, None] * (idx + 1)[None, :])
    chunk_dec = jnp.exp(log_gamma * c)[None, :, None, None]
    kv = jnp.einsum('bhncd,bhnce->nbhde', kc * k_dec[None, :, None, :, None], vc)

    def step(state, kv_n):
        return state * chunk_dec + kv_n, state

    _, prev = lax.scan(step, jnp.zeros((b, h, dk, dv), jnp.float32), kv)
    out = out + jnp.einsum('bhncd,nbhde->bhnce', qc * q_dec[None, :, None, :, None], prev)
    return out.reshape(b, h, t, dv)


def memory_attention(mq, mem_k, mem_v):
    b, s, _ = mq.shape
    qh = mq.reshape(b, s, MEM_HEADS, MEM_HEAD_DIM)
    sc = jnp.einsum('bthd,blhd->bhtl', qh, mem_k).astype(jnp.float32) * (MEM_HEAD_DIM ** -0.5)
    p = jax.nn.softmax(sc, axis=-1).astype(mem_v.dtype)
    o = jnp.einsum('bhtl,blhd->bthd', p, mem_v)
    return o.reshape(b, s, MEM_WIDTH)


def retention_layer(a, mem_k, mem_v, w_in, w_out, gn_g, gn_b, decay):
    b, s, _ = a.shape
    proj = a @ w_in
    q, k, v, g, mq = jnp.split(proj, [TOK_WIDTH, 2 * TOK_WIDTH, 3 * TOK_WIDTH, 4 * TOK_WIDTH], axis=-1)
    pos = jnp.arange(s, dtype=jnp.float32)
    qh = rotary(to_heads(q, RET_HEADS).astype(jnp.float32), pos)
    kh = rotary(to_heads(k, RET_HEADS).astype(jnp.float32), pos) * (RET_HEAD_DIM ** -0.5)
    vh = to_heads(v, RET_HEADS).astype(jnp.float32)
    log_gamma = -jnp.exp(decay.astype(jnp.float32))
    y_fwd = retention_one_direction(qh, kh, vh, log_gamma[0])
    y_bwd = jnp.flip(retention_one_direction(jnp.flip(qh, 2), jnp.flip(kh, 2), jnp.flip(vh, 2),
                                             log_gamma[1]), 2)
    y = y_fwd + y_bwd
    mu = jnp.mean(y, axis=-1, keepdims=True)
    var = jnp.mean(jnp.square(y - mu), axis=-1, keepdims=True)
    y = from_heads((y - mu) * lax.rsqrt(var + GN_EPS))
    y = y * gn_g.astype(jnp.float32) + gn_b.astype(jnp.float32)
    y = (jax.nn.silu(g.astype(jnp.float32)) * y).astype(a.dtype)
    m = memory_attention(mq, mem_k, mem_v)
    return jnp.concatenate([y, m], axis=-1) @ w_out


def neighbourhood_attention(q, k, v, rpb):
    b, h, t, d = q.shape
    rows = t // GRID_W
    wr = min(NA_WIN_R, rows)
    wc = NA_WIN_C
    n_cb = GRID_W // NA_QB
    qg = q.reshape(b, h, rows, n_cb, NA_QB, d) * (d ** -0.5)
    kg = k.reshape(b, h, rows, GRID_W, d)
    vg = v.reshape(b, h, rows, GRID_W, d)
    qcol = np.arange(GRID_W).reshape(n_cb, NA_QB)
    kstart = np.clip(np.arange(n_cb) * NA_QB - NA_WIN_C // 2, 0, GRID_W - NA_KB)
    kcol = kstart[:, None] + np.arange(NA_KB)[None, :]
    wstart = np.clip(qcol - wc // 2, 0, GRID_W - wc)
    col_valid = (kcol[:, None, :] >= wstart[:, :, None]) & (kcol[:, None, :] < wstart[:, :, None] + wc)
    dc_idx = np.clip(kcol[:, None, :] - qcol[:, :, None] + NA_WIN_C - 1, 0, 2 * NA_WIN_C - 2)
    mask = jnp.asarray(col_valid)[:, :, None, :]
    rpb_c = rpb[:, :, dc_idx].astype(jnp.float32)

    def one_row(r):
        rs = jnp.clip(r - wr // 2, 0, rows - wr)
        k_blk = lax.dynamic_slice_in_dim(kg, rs, wr, axis=2)[:, :, :, kcol]
        v_blk = lax.dynamic_slice_in_dim(vg, rs, wr, axis=2)[:, :, :, kcol]
        q_row = lax.dynamic_index_in_dim(qg, r, axis=2, keepdims=False)
        s = jnp.einsum('bhjqd,bhrjkd->bhjqrk', q_row, k_blk).astype(jnp.float32)
        dr_idx = rs + jnp.arange(wr) - r + NA_WIN_R - 1
        bias = jnp.take(rpb_c, dr_idx, axis=1).transpose(0, 2, 3, 1, 4)
        s = jnp.where(mask, s + bias, -1e30)
        p = jax.nn.softmax(s, axis=(-2, -1)).astype(v.dtype)
        return jnp.einsum('bhjqrk,bhrjkd->bhjqd', p, v_blk)

    out = lax.map(one_row, jnp.arange(rows))
    return out.transpose(1, 2, 0, 3, 4, 5).reshape(b, h, t, d)


def na_layer(a, mem_k, mem_v, w_in, w_out, rpb):
    proj = a @ w_in
    q, k, v, mq = jnp.split(proj, [TOK_WIDTH, 2 * TOK_WIDTH, 3 * TOK_WIDTH], axis=-1)
    o = neighbourhood_attention(to_heads(q, NA_HEADS), to_heads(k, NA_HEADS), to_heads(v, NA_HEADS), rpb)
    m = memory_attention(mq, mem_k, mem_v)
    return jnp.concatenate([from_heads(o), m], axis=-1) @ w_out


def setup_inputs(seed: int = 0) -> dict:
    key = jax.random.key(seed)
    ks = jax.random.split(key, 16)
    f32 = jnp.float32
    x = jax.random.normal(ks[0], (BATCH, SEQ, D_MODEL), f32)
    mem = jax.random.normal(ks[1], (BATCH, MEM_LEN, D_MODEL), f32)
    norm_g = 1.0 + 0.02 * jax.random.normal(ks[2], (DEPTH, 4, D_MODEL), f32)
    mem_norm_g = 1.0 + 0.02 * jax.random.normal(ks[3], (D_MODEL,), f32)
    mem_w_kv = jax.random.normal(ks[4], (D_MODEL, 2 * MEM_WIDTH), f32) * D_MODEL ** -0.5
    ret_w_in = jax.random.normal(ks[5], (N_RET_LAYERS, D_MODEL, 4 * TOK_WIDTH + MEM_WIDTH), f32) * D_MODEL ** -0.5
    ret_w_out = jax.random.normal(ks[6], (N_RET_LAYERS, MIX_WIDTH, D_MODEL), f32) * MIX_WIDTH ** -0.5
    ret_gn_g = 1.0 + 0.02 * jax.random.normal(ks[7], (N_RET_LAYERS, TOK_WIDTH), f32)
    ret_gn_b = 0.02 * jax.random.normal(ks[8], (N_RET_LAYERS, TOK_WIDTH), f32)
    base = jnp.log(-jnp.log(1.0 - 2.0 ** (-5.0 - jnp.arange(RET_HEADS, dtype=f32))))
    ret_decay = base[None, None, :] + 0.05 * jax.random.normal(ks[9], (N_RET_LAYERS, 2, RET_HEADS), f32)
    na_w_in = jax.random.normal(ks[10], (N_NA_LAYERS, D_MODEL, 3 * TOK_WIDTH + MEM_WIDTH), f32) * D_MODEL ** -0.5
    na_w_out = jax.random.normal(ks[11], (N_NA_LAYERS, MIX_WIDTH, D_MODEL), f32) * MIX_WIDTH ** -0.5
    na_rpb = 0.1 * jax.random.normal(ks[12], (N_NA_LAYERS, NA_HEADS, 2 * NA_WIN_R - 1, 2 * NA_WIN_C - 1), f32)
    ffn_w_in = jax.random.normal(ks[13], (DEPTH, D_MODEL, 2 * FFN_HIDDEN), f32) * D_MODEL ** -0.5
    ffn_w_out = jax.random.normal(ks[14], (DEPTH, FFN_HIDDEN, D_MODEL), f32) * FFN_HIDDEN ** -0.5
    return {"x": x, "mem": mem, "norm_g": norm_g, "mem_norm_g": mem_norm_g, "mem_w_kv": mem_w_kv,
            "ret_w_in": ret_w_in, "ret_w_out": ret_w_out, "ret_gn_g": ret_gn_g, "ret_gn_b": ret_gn_b,
            "ret_decay": ret_decay, "na_w_in": na_w_in, "na_w_out": na_w_out, "na_rpb": na_rpb,
            "ffn_w_in": ffn_w_in, "ffn_w_out": ffn_w_out}


def reference(x, mem, norm_g, mem_norm_g, mem_w_kv, ret_w_in, ret_w_out, ret_gn_g, ret_gn_b,
              ret_decay, na_w_in, na_w_out, na_rpb, ffn_w_in, ffn_w_out):
    b, l, _ = mem.shape
    mkv = rms_norm(mem, mem_norm_g) @ mem_w_kv
    mem_k, mem_v = jnp.split(mkv, 2, axis=-1)
    mem_k = mem_k.reshape(b, l, MEM_HEADS, MEM_HEAD_DIM)
    mem_v = mem_v.reshape(b, l, MEM_HEADS, MEM_HEAD_DIM)
    h = x
    for i in range(DEPTH):
        j = i // N_MIXERS
        a = rms_norm(h, norm_g[i, 0])
        if i % N_MIXERS == 0:
            y = retention_layer(a, mem_k, mem_v, ret_w_in[j], ret_w_out[j], ret_gn_g[j], ret_gn_b[j], ret_decay[j])
        else:
            y = na_layer(a, mem_k, mem_v, na_w_in[j], na_w_out[j], na_rpb[j])
        h = h + rms_norm(y, norm_g[i, 1])
        a = rms_norm(h, norm_g[i, 2])
        gate, up = jnp.split(a @ ffn_w_in[i], 2, axis=-1)
        f = (jax.nn.silu(gate) * up) @ ffn_w_out[i]
        h = h + rms_norm(f, norm_g[i, 3])
    return h
```

```python
import functools

import numpy as np
import jax
import jax.numpy as jnp
from jax import lax
from jax.experimental import pallas as pl
from jax.experimental.pallas import tpu as pltpu

F32 = jnp.float32
BF16 = jnp.bfloat16

LANES = 128
MEM_HEADS = 4
MEM_HEAD_DIM = 64
MEM_WIDTH = MEM_HEADS * MEM_HEAD_DIM
RET_HEADS = 6
RET_HEAD_DIM = 128
RET_CHUNK = 128
RET_ROPE_BASE = 10000.0
GN_EPS = 1e-5
NA_HEADS = 12
NA_HEAD_DIM = 64
GRID_W = 64
NA_WIN_R = 8
NA_WIN_C = 16
NA_ROWS_PER_STEP = 2
EPS = 1e-6
NEG = -1e30
VMEM_LIMIT = 56 * 1024 * 1024

_NT = (((1,), (1,)), ((), ()))
_TN = (((0,), (0,)), ((), ()))


def _rms(x, g):
    return x * lax.rsqrt(jnp.mean(x * x, axis=-1, keepdims=True) + EPS) * g


def _const_spec(shape):
    nd = len(shape)
    return pl.BlockSpec(shape, lambda *_: (0,) * nd, pipeline_mode=pl.Buffered(1))


def _mem_kv_kernel(mem_ref, g_ref, w_ref, k_ref, v_ref):
    a = _rms(mem_ref[...], g_ref[...]).astype(BF16)
    kv = jnp.dot(a, w_ref[...], preferred_element_type=F32)
    k_ref[...] = kv[:, :MEM_WIDTH].astype(BF16)
    v_ref[...] = kv[:, MEM_WIDTH:].astype(BF16)


def _mem_kv(mem2d, g, w_kv, mem_len):
    m, d = mem2d.shape
    return pl.pallas_call(
        _mem_kv_kernel,
        grid=(m // mem_len,),
        in_specs=[pl.BlockSpec((mem_len, d), lambda i: (i, 0)),
                  _const_spec((1, d)),
                  _const_spec((d, 2 * MEM_WIDTH))],
        out_specs=[pl.BlockSpec((mem_len, MEM_WIDTH), lambda i: (i, 0))] * 2,
        out_shape=[jax.ShapeDtypeStruct((m, MEM_WIDTH), BF16)] * 2,
        name="mem_kv",
    )(mem2d, g.reshape(1, d), w_kv)


def _in_proj_kernel(h_ref, g_ref, w_ref, *rest, n_rot):
    if n_rot:
        cos_ref, sin_ref, o_ref = rest
    else:
        (o_ref,) = rest
    a = _rms(h_ref[...], g_ref[...]).astype(BF16)
    n = o_ref.shape[1]
    step = 2 * LANES
    for c in range(n // step):
        y = jnp.dot(a, w_ref[:, c * step:(c + 1) * step], preferred_element_type=F32)
        if 2 * c < n_rot:
            halves = []
            for y1 in (y[:, :LANES], y[:, LANES:]):
                halves.append(y1 * cos_ref[...] + pltpu.roll(y1, LANES // 2, 1) * sin_ref[...])
            y = jnp.concatenate(halves, axis=1)
        o_ref[:, c * step:(c + 1) * step] = y.astype(BF16)


def _in_proj(h, g, w, rot, tm, seq):
    m, d = h.shape
    n = w.shape[1]
    in_specs = [pl.BlockSpec((tm, d), lambda i: (i, 0)), _const_spec((1, d)), _const_spec((d, n))]
    args = [h, g.reshape(1, d), w]
    n_rot = 0
    if rot is not None:
        n_rot = 2 * RET_HEADS
        tiles_per_seq = seq // tm
        in_specs += [pl.BlockSpec((tm, LANES), lambda i: (i % tiles_per_seq, 0))] * 2
        args += list(rot)
    return pl.pallas_call(
        functools.partial(_in_proj_kernel, n_rot=n_rot),
        grid=(m // tm,),
        in_specs=in_specs,
        out_specs=pl.BlockSpec((tm, n), lambda i: (i, 0)),
        out_shape=jax.ShapeDtypeStruct((m, n), BF16),
        compiler_params=pltpu.CompilerParams(dimension_semantics=("arbitrary",),
                                             vmem_limit_bytes=VMEM_LIMIT),
        name="in_proj",
    )(*args)


def _rotary_tables(seq):
    half = RET_HEAD_DIM // 2
    inv = RET_ROPE_BASE ** (-jnp.arange(half, dtype=F32) / half)
    ang = jnp.arange(seq, dtype=F32)[:, None] * inv[None, :]
    c, s = jnp.cos(ang), jnp.sin(ang)
    return jnp.concatenate([c, c], axis=1), jnp.concatenate([-s, s], axis=1)


def _ret_kernel(lg_ref, q_ref, k_ref, v_ref, g_ref, gng_ref, gnb_ref, o_ref, sf_ref, sb_ref):
    hd = pl.program_id(1)
    c = RET_CHUNK
    n_chunks = q_ref.shape[0] // c
    lg_f = lg_ref[0, hd]
    lg_b = lg_ref[1, hd]
    scale = RET_HEAD_DIM ** -0.5

    row = lax.broadcasted_iota(jnp.int32, (c, c), 0)
    col = lax.broadcasted_iota(jnp.int32, (c, c), 1)
    rel = (row - col).astype(F32)
    decay = (jnp.where(rel >= 0, jnp.exp(lg_f * jnp.maximum(rel, 0.0)), 0.0)
             + jnp.where(rel <= 0, jnp.exp(lg_b * jnp.maximum(-rel, 0.0)), 0.0)) * scale
    idx = row.astype(F32)
    kdec_f = jnp.exp(lg_f * (c - 1 - idx))
    kdec_b = jnp.exp(lg_b * idx)
    qdec_f = jnp.exp(lg_f * (idx + 1)) * scale
    qdec_b = jnp.exp(lg_b * (c - idx)) * scale
    cd_f = jnp.exp(jnp.full((1, 1), c, F32) * lg_f)
    cd_b = jnp.exp(jnp.full((1, 1), c, F32) * lg_b)

    def chunk(n):
        return pl.ds(pl.multiple_of(n * c, c), c)

    def kv_update(n, state, kdec, cd):
        kd = (k_ref[chunk(n), :].astype(F32) * kdec).astype(BF16)
        kv = lax.dot_general(kd, v_ref[chunk(n), :], _TN, preferred_element_type=F32)
        return state * cd + kv

    def fwd_body(n, state):
        sf_ref[n] = state.astype(BF16)
        return kv_update(n, state, kdec_f, cd_f)

    def bwd_body(i, state):
        n = n_chunks - 1 - i
        sb_ref[n] = state.astype(BF16)
        return kv_update(n, state, kdec_b, cd_b)

    zero = jnp.zeros((RET_HEAD_DIM, RET_HEAD_DIM), F32)
    lax.fori_loop(0, n_chunks, fwd_body, zero)
    lax.fori_loop(0, n_chunks, bwd_body, zero)

    def out_body(n, carry):
        q = q_ref[chunk(n), :]
        qf32 = q.astype(F32)
        s = lax.dot_general(q, k_ref[chunk(n), :], _NT, preferred_element_type=F32) * decay
        y = jnp.dot(s.astype(BF16), v_ref[chunk(n), :], preferred_element_type=F32)
        y += jnp.dot((qf32 * qdec_f).astype(BF16), sf_ref[n], preferred_element_type=F32)
        y += jnp.dot((qf32 * qdec_b).astype(BF16), sb_ref[n], preferred_element_type=F32)
        mu = jnp.mean(y, axis=-1, keepdims=True)
        yc = y - mu
        var = jnp.mean(yc * yc, axis=-1, keepdims=True)
        yn = yc * lax.rsqrt(var + GN_EPS) * gng_ref[...] + gnb_ref[...]
        gate = g_ref[chunk(n), :].astype(F32)
        o_ref[chunk(n), :] = (gate * jax.nn.sigmoid(gate) * yn).astype(BF16)
        return carry

    lax.fori_loop(0, n_chunks, out_body, 0)


def _retention(proj, log_gamma, gn_g, gn_b, batch, seq):
    m, _ = proj.shape
    n_chunks = seq // RET_CHUNK
    hblk = lambda off: pl.BlockSpec((seq, RET_HEAD_DIM), lambda b, h: (b, off + h))
    vec = pl.BlockSpec((1, RET_HEAD_DIM), lambda b, h: (0, h))
    tok = RET_HEADS * RET_HEAD_DIM
    return pl.pallas_call(
        _ret_kernel,
        grid=(batch, RET_HEADS),
        in_specs=[pl.BlockSpec(memory_space=pltpu.SMEM),
                  hblk(0), hblk(RET_HEADS), hblk(2 * RET_HEADS), hblk(3 * RET_HEADS), vec, vec],
        out_specs=pl.BlockSpec((seq, RET_HEAD_DIM), lambda b, h: (b, h)),
        out_shape=jax.ShapeDtypeStruct((m, tok), BF16),
        scratch_shapes=[pltpu.VMEM((n_chunks, RET_HEAD_DIM, RET_HEAD_DIM), BF16)] * 2,
        compiler_params=pltpu.CompilerParams(dimension_semantics=("arbitrary", "arbitrary"),
                                             vmem_limit_bytes=VMEM_LIMIT),
        name="retention",
    )(log_gamma, proj, proj, proj, proj, gn_g.reshape(1, tok), gn_b.reshape(1, tok))


def _na_step_geometry(rows):
    rg = NA_ROWS_PER_STEP
    wr = min(NA_WIN_R, rows)
    win = wr + rg - 1
    steps = rows // rg
    starts = [int(np.clip(np.clip(s * rg - wr // 2, 0, rows - wr), 0, rows - win)) for s in range(steps)]
    keys = [(starts[s] - s * rg) for s in range(steps)]
    reps, types = [], []
    for s in range(steps):
        sig = (keys[s], tuple(int(np.clip(s * rg + j - wr // 2, 0, rows - wr)) - s * rg for j in range(rg)))
        for t, (sig_t, _) in enumerate(reps):
            if sig_t == sig:
                types.append(t)
                break
        else:
            types.append(len(reps))
            reps.append((sig, s))
    return starts, types, [s for _, s in reps], win


def _na_bias_tables(rpb, rows):
    rg = NA_ROWS_PER_STEP
    wr = min(NA_WIN_R, rows)
    starts, _, rep_steps, win = _na_step_geometry(rows)
    dr_all, dc_all, ok_all = [], [], []
    for s in rep_steps:
        qrow = (s * rg + np.arange(rg))[:, None, None, None]
        qcol = np.arange(GRID_W)[None, :, None, None]
        krow = (starts[s] + np.arange(win))[None, None, :, None]
        kcol = np.arange(GRID_W)[None, None, None, :]
        rs = np.clip(qrow - wr // 2, 0, rows - wr)
        cs = np.clip(qcol - NA_WIN_C // 2, 0, GRID_W - NA_WIN_C)
        ok = (krow >= rs) & (krow < rs + wr) & (kcol >= cs) & (kcol < cs + NA_WIN_C)
        dr = np.clip(krow - qrow + NA_WIN_R - 1, 0, 2 * NA_WIN_R - 2)
        dc = np.clip(kcol - qcol + NA_WIN_C - 1, 0, 2 * NA_WIN_C - 2)
        shape = (rg * GRID_W, win * GRID_W)
        dr_all.append(np.broadcast_to(dr, ok.shape).reshape(shape))
        dc_all.append(np.broadcast_to(dc, ok.shape).reshape(shape))
        ok_all.append(ok.reshape(shape))
    dr_all, dc_all, ok_all = np.stack(dr_all), np.stack(dc_all), np.stack(ok_all)
    bias = rpb.astype(F32)[:, dr_all, dc_all]
    return jnp.where(jnp.asarray(ok_all)[None], bias, NEG)


def _na_kernel(start_ref, type_ref, q_ref, k_ref, v_ref, tab_ref, o_ref, *, win):
    nq = NA_ROWS_PER_STEP * GRID_W
    nk = win * GRID_W
    steps = q_ref.shape[0] // nq
    lo = lax.broadcasted_iota(jnp.int32, (1, LANES), 1) < NA_HEAD_DIM
    scale = NA_HEAD_DIM ** -0.5

    def body(s, carry):
        q = q_ref[pl.ds(pl.multiple_of(s * nq, nq), nq), :]
        koff = pl.multiple_of(start_ref[s] * GRID_W, GRID_W)
        kw = k_ref[pl.ds(koff, nk), :]
        vw = v_ref[pl.ds(koff, nk), :]
        typ = type_ref[s]
        qs = q * scale
        outs = []
        for hd, sel in enumerate((lo, jnp.logical_not(lo))):
            qh = jnp.where(sel, qs, jnp.zeros_like(qs))
            sc = lax.dot_general(qh, kw, _NT, preferred_element_type=F32) + tab_ref[hd, typ]
            sc = sc - jnp.max(sc, axis=-1, keepdims=True)
            p = jnp.exp(sc)
            l = jnp.sum(p, axis=-1, keepdims=True)
            outs.append(jnp.dot(p.astype(BF16), vw, preferred_element_type=F32) / l)
        o_ref[pl.ds(pl.multiple_of(s * nq, nq), nq), :] = jnp.where(lo, outs[0], outs[1]).astype(BF16)
        return carry

    lax.fori_loop(0, steps, body, 0)


def _neighbourhood(proj, tables, batch, seq):
    m, _ = proj.shape
    rows = seq // GRID_W
    starts, types, _, win = _na_step_geometry(rows)
    pairs = NA_HEADS // 2
    tok = NA_HEADS * NA_HEAD_DIM
    n_types, nq, nk = tables.shape[1:]
    tables = tables.reshape(pairs, 2, n_types, nq, nk)
    hblk = lambda off: pl.BlockSpec((seq, LANES), lambda b, p: (b, off + p))
    smem = pl.BlockSpec(memory_space=pltpu.SMEM)
    return pl.pallas_call(
        functools.partial(_na_kernel, win=win),
        grid=(batch, pairs),
        in_specs=[smem, smem, hblk(0), hblk(pairs), hblk(2 * pairs),
                  pl.BlockSpec((None, 2, n_types, nq, nk), lambda b, p: (p, 0, 0, 0, 0))],
        out_specs=pl.BlockSpec((seq, LANES), lambda b, p: (b, p)),
        out_shape=jax.ShapeDtypeStruct((m, tok), BF16),
        compiler_params=pltpu.CompilerParams(dimension_semantics=("arbitrary", "arbitrary"),
                                             vmem_limit_bytes=VMEM_LIMIT),
        name="neighbourhood",
    )(jnp.asarray(starts, jnp.int32), jnp.asarray(types, jnp.int32), proj, proj, proj, tables)


def _post_kernel(y_ref, mq_ref, mk_ref, mv_ref, h_ref, wo_ref, g_ref, w1_ref, w2_ref, o_ref, f_ref):
    tok = y_ref.shape[1]
    lane = lax.broadcasted_iota(jnp.int32, (1, MEM_WIDTH), 1)
    mq = mq_ref[...] * (MEM_HEAD_DIM ** -0.5)
    mk = mk_ref[...]
    mv = mv_ref[...]
    m = jnp.zeros(mq.shape, F32)
    for hd in range(MEM_HEADS):
        sel = (lane >= hd * MEM_HEAD_DIM) & (lane < (hd + 1) * MEM_HEAD_DIM)
        qh = jnp.where(sel, mq, jnp.zeros_like(mq))
        sc = lax.dot_general(qh, mk, _NT, preferred_element_type=F32)
        sc = sc - jnp.max(sc, axis=-1, keepdims=True)
        p = jnp.exp(sc)
        l = jnp.sum(p, axis=-1, keepdims=True)
        vh = jnp.where(sel, mv, jnp.zeros_like(mv))
        m += jnp.dot(p.astype(BF16), vh, preferred_element_type=F32) / l

    y = jnp.dot(y_ref[...], wo_ref[:tok, :], preferred_element_type=F32)
    y += jnp.dot(m.astype(BF16), wo_ref[tok:, :], preferred_element_type=F32)
    h1 = h_ref[...] + _rms(y, g_ref[0:1, :])
    a = _rms(h1, g_ref[1:2, :]).astype(BF16)

    f_ref[...] = jnp.zeros_like(f_ref)

    def ffn_body(c, carry):
        gate = jnp.dot(a, w1_ref[0, c], preferred_element_type=F32)
        up = jnp.dot(a, w1_ref[1, c], preferred_element_type=F32)
        act = (gate * jax.nn.sigmoid(gate) * up).astype(BF16)
        f_ref[...] += jnp.dot(act, w2_ref[c], preferred_element_type=F32)
        return carry

    lax.fori_loop(0, w2_ref.shape[0], ffn_body, 0)
    o_ref[...] = h1 + _rms(f_ref[...], g_ref[2:3, :])


def _post(y, proj, mq_block, mem_k, mem_v, h, w_out, gains, w1, w2, tm, seq, mem_len):
    m, d = h.shape
    tok = y.shape[1]
    tiles_per_seq = seq // tm
    n_chunks, hc = w2.shape[0], w2.shape[1]
    return pl.pallas_call(
        _post_kernel,
        grid=(m // tm,),
        in_specs=[pl.BlockSpec((tm, tok), lambda i: (i, 0)),
                  pl.BlockSpec((tm, MEM_WIDTH), lambda i: (i, mq_block)),
                  pl.BlockSpec((mem_len, MEM_WIDTH), lambda i: (i // tiles_per_seq, 0)),
                  pl.BlockSpec((mem_len, MEM_WIDTH), lambda i: (i // tiles_per_seq, 0)),
                  pl.BlockSpec((tm, d), lambda i: (i, 0)),
                  _const_spec((d, d)),
                  _const_spec((3, d)),
                  _const_spec((2, n_chunks, d, hc)),
                  _const_spec((n_chunks, hc, d))],
        out_specs=pl.BlockSpec((tm, d), lambda i: (i, 0)),
        out_shape=jax.ShapeDtypeStruct((m, d), F32),
        scratch_shapes=[pltpu.VMEM((tm, d), F32)],
        compiler_params=pltpu.CompilerParams(dimension_semantics=("arbitrary",),
                                             vmem_limit_bytes=VMEM_LIMIT),
        name="post",
    )(y, proj, mem_k, mem_v, h, w_out, gains, w1, w2)


FFN_CHUNK = 256


def kernel(x, mem, norm_g, mem_norm_g, mem_w_kv, ret_w_in, ret_w_out, ret_gn_g, ret_gn_b, ret_decay,
           na_w_in, na_w_out, na_rpb, ffn_w_in, ffn_w_out):
    batch, seq, d = x.shape
    mem_len = mem.shape[1]
    depth = norm_g.shape[0]
    hidden = ffn_w_out.shape[1]
    tm = min(512, seq)
    rows = seq // GRID_W

    mem_k, mem_v = _mem_kv(mem.reshape(batch * mem_len, d), mem_norm_g, mem_w_kv.astype(BF16), mem_len)
    rot = _rotary_tables(seq)
    h = x.reshape(batch * seq, d)
    for i in range(depth):
        j = i // 2
        if i % 2 == 0:
            proj = _in_proj(h, norm_g[i, 0], ret_w_in[j].astype(BF16), rot, tm, seq)
            log_gamma = -jnp.exp(ret_decay[j].astype(F32))
            y = _retention(proj, log_gamma, ret_gn_g[j], ret_gn_b[j], batch, seq)
            w_out = ret_w_out[j]
        else:
            proj = _in_proj(h, norm_g[i, 0], na_w_in[j].astype(BF16), None, tm, seq)
            y = _neighbourhood(proj, _na_bias_tables(na_rpb[j], rows), batch, seq)
            w_out = na_w_out[j]
        mq_block = (proj.shape[1] - MEM_WIDTH) // MEM_WIDTH
        n_chunks = hidden // FFN_CHUNK
        w1 = ffn_w_in[i].astype(BF16).reshape(d, 2, n_chunks, FFN_CHUNK).transpose(1, 2, 0, 3)
        w2 = ffn_w_out[i].astype(BF16).reshape(n_chunks, FFN_CHUNK, d)
        h = _post(y, proj, mq_block, mem_k, mem_v, h, w_out.astype(BF16), norm_g[i, 1:4], w1, w2,
                  tm, seq, mem_len)
    return h.reshape(batch, seq, d)
```

```python
import functools

import numpy as np
import jax
import jax.numpy as jnp
from jax import lax
from jax.experimental import pallas as pl
from jax.experimental.pallas import tpu as pltpu

F32 = jnp.float32
BF16 = jnp.bfloat16

LANES = 128
MEM_HEADS = 4
MEM_HEAD_DIM = 64
MEM_WIDTH = MEM_HEADS * MEM_HEAD_DIM
RET_HEADS = 6
RET_HEAD_DIM = 128
RET_CHUNK = 128
RET_ROPE_BASE = 10000.0
RET_UNROLL = 4
GN_EPS = 1e-5
NA_HEADS = 12
NA_HEAD_DIM = 64
GRID_W = 64
NA_WIN_R = 8
NA_WIN_C = 16
NA_ROWS_PER_STEP = 2
NA_UNROLL = 2
EPS = 1e-6
NEG = -1e30
VMEM_LIMIT = 56 * 1024 * 1024

_NT = (((1,), (1,)), ((), ()))


def _rms(x, g):
    return x * lax.rsqrt(jnp.mean(x * x, axis=-1, keepdims=True) + EPS) * g


def _const_spec(shape):
    nd = len(shape)
    return pl.BlockSpec(shape, lambda *_: (0,) * nd, pipeline_mode=pl.Buffered(1))


def _mem_kv_kernel(mem_ref, g_ref, w_ref, k_ref, v_ref):
    a = _rms(mem_ref[...], g_ref[...]).astype(BF16)
    kv = jnp.dot(a, w_ref[...], preferred_element_type=F32)
    k_ref[...] = kv[:, :MEM_WIDTH].astype(BF16)
    v_ref[...] = kv[:, MEM_WIDTH:].astype(BF16)


def _mem_kv(mem2d, g, w_kv, mem_len):
    m, d = mem2d.shape
    return pl.pallas_call(
        _mem_kv_kernel,
        grid=(m // mem_len,),
        in_specs=[pl.BlockSpec((mem_len, d), lambda i: (i, 0)),
                  _const_spec((1, d)),
                  _const_spec((d, 2 * MEM_WIDTH))],
        out_specs=[pl.BlockSpec((mem_len, MEM_WIDTH), lambda i: (i, 0))] * 2,
        out_shape=[jax.ShapeDtypeStruct((m, MEM_WIDTH), BF16)] * 2,
        name="mem_kv",
    )(mem2d, g.reshape(1, d), w_kv)


def _in_proj_kernel(h_ref, g_ref, w_ref, *rest, n_rot):
    if n_rot:
        cos_ref, sin_ref, o_ref = rest
    else:
        (o_ref,) = rest
    a = _rms(h_ref[...], g_ref[...]).astype(BF16)
    n = o_ref.shape[1]
    step = 2 * LANES
    for c in range(n // step):
        y = jnp.dot(a, w_ref[:, c * step:(c + 1) * step], preferred_element_type=F32)
        if 2 * c < n_rot:
            halves = []
            for y1 in (y[:, :LANES], y[:, LANES:]):
                halves.append(y1 * cos_ref[...] + pltpu.roll(y1, LANES // 2, 1) * sin_ref[...])
            y = jnp.concatenate(halves, axis=1)
        o_ref[:, c * step:(c + 1) * step] = y.astype(BF16)


def _in_proj(h, g, w, rot, tm, seq):
    m, d = h.shape
    n = w.shape[1]
    in_specs = [pl.BlockSpec((tm, d), lambda i: (i, 0)), _const_spec((1, d)), _const_spec((d, n))]
    args = [h, g.reshape(1, d), w]
    n_rot = 0
    if rot is not None:
        n_rot = 2 * RET_HEADS
        tiles_per_seq = seq // tm
        in_specs += [pl.BlockSpec((tm, LANES), lambda i: (i % tiles_per_seq, 0))] * 2
        args += list(rot)
    return pl.pallas_call(
        functools.partial(_in_proj_kernel, n_rot=n_rot),
        grid=(m // tm,),
        in_specs=in_specs,
        out_specs=pl.BlockSpec((tm, n), lambda i: (i, 0)),
        out_shape=jax.ShapeDtypeStruct((m, n), BF16),
        compiler_params=pltpu.CompilerParams(dimension_semantics=("arbitrary",),
                                             vmem_limit_bytes=VMEM_LIMIT),
        name="in_proj",
    )(*args)


def _rotary_tables(seq):
    half = RET_HEAD_DIM // 2
    inv = RET_ROPE_BASE ** (-jnp.arange(half, dtype=F32) / half)
    ang = jnp.arange(seq, dtype=F32)[:, None] * inv[None, :]
    c, s = jnp.cos(ang), jnp.sin(ang)
    return jnp.concatenate([c, c], axis=1), jnp.concatenate([-s, s], axis=1)


def _ret_kernel(lg_ref, q_ref, k_ref, v_ref, g_ref, gng_ref, gnb_ref, o_ref, kt_ref, kv_ref, st_ref):
    hd = pl.program_id(1)
    c = RET_CHUNK
    dk = RET_HEAD_DIM
    n_chunks = q_ref.shape[0] // c
    lg_f = lg_ref[0, hd]
    lg_b = lg_ref[1, hd]
    scale = RET_HEAD_DIM ** -0.5

    row = lax.broadcasted_iota(jnp.int32, (c, c), 0)
    col = lax.broadcasted_iota(jnp.int32, (c, c), 1)
    rel = (row - col).astype(F32)
    decay = (jnp.where(rel >= 0, jnp.exp(lg_f * jnp.maximum(rel, 0.0)), 0.0)
             + jnp.where(rel <= 0, jnp.exp(lg_b * jnp.maximum(-rel, 0.0)), 0.0)) * scale
    tok_row = row.astype(F32)
    tok_col = col.astype(F32)
    cd_f = jnp.exp(jnp.full((1, 1), c, F32) * lg_f)
    cd_b = jnp.exp(jnp.full((1, 1), c, F32) * lg_b)

    def chunk(n):
        return pl.ds(pl.multiple_of(n * c, c), c)

    def transpose_body(n, carry):
        kt_ref[:, chunk(n)] = k_ref[chunk(n), :].astype(F32).T.astype(BF16)
        return carry

    lax.fori_loop(0, n_chunks, transpose_body, 0, unroll=RET_UNROLL)

    kdec_f = jnp.exp(lg_f * (c - 1 - tok_col))
    kdec_b = jnp.exp(lg_b * tok_col)

    def kv_body(n, carry):
        kt = kt_ref[:, chunk(n)].astype(F32)
        kd = jnp.concatenate([(kt * kdec_f).astype(BF16), (kt * kdec_b).astype(BF16)], axis=0)
        kv_ref[n] = jnp.dot(kd, v_ref[chunk(n), :], preferred_element_type=F32)
        return carry

    lax.fori_loop(0, n_chunks, kv_body, 0, unroll=RET_UNROLL)

    def scan_body(i, states):
        s_f, s_b = states
        j = n_chunks - 1 - i
        st_ref[i, :dk, :] = s_f.astype(BF16)
        st_ref[j, dk:, :] = s_b.astype(BF16)
        return s_f * cd_f + kv_ref[i, :dk, :], s_b * cd_b + kv_ref[j, dk:, :]

    zero = jnp.zeros((dk, RET_HEAD_DIM), F32)
    lax.fori_loop(0, n_chunks, scan_body, (zero, zero), unroll=RET_UNROLL)

    qdec_f = jnp.exp(lg_f * (tok_row + 1)) * scale
    qdec_b = jnp.exp(lg_b * (c - tok_row)) * scale

    def out_body(n, carry):
        q = q_ref[chunk(n), :]
        qf32 = q.astype(F32)
        s = jnp.dot(q, kt_ref[:, chunk(n)], preferred_element_type=F32) * decay
        y = jnp.dot(s.astype(BF16), v_ref[chunk(n), :], preferred_element_type=F32)
        qd = jnp.concatenate([(qf32 * qdec_f).astype(BF16), (qf32 * qdec_b).astype(BF16)], axis=1)
        y += jnp.dot(qd, st_ref[n], preferred_element_type=F32)
        mu = jnp.mean(y, axis=-1, keepdims=True)
        yc = y - mu
        var = jnp.mean(yc * yc, axis=-1, keepdims=True)
        yn = yc * lax.rsqrt(var + GN_EPS) * gng_ref[...] + gnb_ref[...]
        gate = g_ref[chunk(n), :].astype(F32)
        o_ref[chunk(n), :] = (gate * jax.nn.sigmoid(gate) * yn).astype(BF16)
        return carry

    lax.fori_loop(0, n_chunks, out_body, 0, unroll=RET_UNROLL)


def _retention(proj, log_gamma, gn_g, gn_b, batch, seq):
    m, _ = proj.shape
    n_chunks = seq // RET_CHUNK
    hblk = lambda off: pl.BlockSpec((seq, RET_HEAD_DIM), lambda b, h: (b, off + h))
    vec = pl.BlockSpec((1, RET_HEAD_DIM), lambda b, h: (0, h))
    tok = RET_HEADS * RET_HEAD_DIM
    return pl.pallas_call(
        _ret_kernel,
        grid=(batch, RET_HEADS),
        in_specs=[pl.BlockSpec(memory_space=pltpu.SMEM),
                  hblk(0), hblk(RET_HEADS), hblk(2 * RET_HEADS), hblk(3 * RET_HEADS), vec, vec],
        out_specs=pl.BlockSpec((seq, RET_HEAD_DIM), lambda b, h: (b, h)),
        out_shape=jax.ShapeDtypeStruct((m, tok), BF16),
        scratch_shapes=[pltpu.VMEM((RET_HEAD_DIM, seq), BF16),
                        pltpu.VMEM((n_chunks, 2 * RET_HEAD_DIM, RET_HEAD_DIM), F32),
                        pltpu.VMEM((n_chunks, 2 * RET_HEAD_DIM, RET_HEAD_DIM), BF16)],
        compiler_params=pltpu.CompilerParams(dimension_semantics=("arbitrary", "arbitrary"),
                                             vmem_limit_bytes=VMEM_LIMIT),
        name="retention",
    )(log_gamma, proj, proj, proj, proj, gn_g.reshape(1, tok), gn_b.reshape(1, tok))


def _na_step_geometry(rows):
    rg = NA_ROWS_PER_STEP
    wr = min(NA_WIN_R, rows)
    win = wr + rg - 1
    steps = rows // rg
    starts = [int(np.clip(np.clip(s * rg - wr // 2, 0, rows - wr), 0, rows - win)) for s in range(steps)]
    reps, types = [], []
    for s in range(steps):
        sig = (starts[s] - s * rg,
               tuple(int(np.clip(s * rg + j - wr // 2, 0, rows - wr)) - s * rg for j in range(rg)))
        for t, (sig_t, _) in enumerate(reps):
            if sig_t == sig:
                types.append(t)
                break
        else:
            types.append(len(reps))
            reps.append((sig, s))
    return starts, types, [s for _, s in reps], win


def _na_bias_tables(rpb, rows):
    rg = NA_ROWS_PER_STEP
    wr = min(NA_WIN_R, rows)
    starts, _, rep_steps, win = _na_step_geometry(rows)
    heads, n_dr, n_dc = rpb.shape
    w = GRID_W
    side = w - 1 - (NA_WIN_C - 1)
    v = jnp.pad(rpb.astype(F32), ((0, 0), (0, 0), (side, side + 1)))
    skew = jnp.broadcast_to(v[:, :, None, :], (heads, n_dr, w, 2 * w)).reshape(heads, n_dr, 2 * w * w)
    skew = skew[:, :, :w * (2 * w - 1)].reshape(heads, n_dr, w, 2 * w - 1)
    band = skew[:, :, :, w - 1:]
    pad = rg + wr
    band = jnp.pad(band, ((0, 0), (pad, pad), (0, 0), (0, 0)))
    parts, ok_all = [], []
    for s in rep_steps:
        per_row = []
        for j in range(rg):
            lo = starts[s] - (s * rg + j) + NA_WIN_R - 1 + pad
            per_row.append(band[:, lo:lo + win])
        parts.append(jnp.stack(per_row, axis=1))
        qrow = (s * rg + np.arange(rg))[:, None, None, None]
        qcol = np.arange(w)[None, :, None, None]
        krow = (starts[s] + np.arange(win))[None, None, :, None]
        kcol = np.arange(w)[None, None, None, :]
        rs = np.clip(qrow - wr // 2, 0, rows - wr)
        cs = np.clip(qcol - NA_WIN_C // 2, 0, w - NA_WIN_C)
        ok = (krow >= rs) & (krow < rs + wr) & (kcol >= cs) & (kcol < cs + NA_WIN_C)
        ok_all.append(ok.reshape(rg * w, win * w))
    tab = jnp.stack(parts, axis=1)
    n_types = len(rep_steps)
    tab = tab.transpose(0, 1, 2, 4, 3, 5).reshape(heads, n_types, rg * w, win * w)
    tab = jnp.where(jnp.asarray(np.stack(ok_all))[None], tab, NEG)
    tab = tab.reshape(heads // 2, 2, n_types, rg * w, win * w).transpose(0, 2, 1, 3, 4)
    return tab.reshape(heads // 2, n_types, 2 * rg * w, win * w)


def _na_kernel(start_ref, type_ref, q_ref, k_ref, v_ref, tab_ref, o_ref, *, win):
    nq = NA_ROWS_PER_STEP * GRID_W
    nk = win * GRID_W
    steps = q_ref.shape[0] // nq
    lo = lax.broadcasted_iota(jnp.int32, (1, LANES), 1) < NA_HEAD_DIM
    scale = NA_HEAD_DIM ** -0.5

    def body(s, carry):
        q = q_ref[pl.ds(pl.multiple_of(s * nq, nq), nq), :] * scale
        koff = pl.multiple_of(start_ref[s] * GRID_W, GRID_W)
        kw = k_ref[pl.ds(koff, nk), :]
        vw = v_ref[pl.ds(koff, nk), :]
        zero = jnp.zeros_like(q)
        qh = jnp.concatenate([jnp.where(lo, q, zero), jnp.where(lo, zero, q)], axis=0)
        sc = lax.dot_general(qh, kw, _NT, preferred_element_type=F32) + tab_ref[type_ref[s]]
        sc = sc - jnp.max(sc, axis=-1, keepdims=True)
        p = jnp.exp(sc)
        l = jnp.sum(p, axis=-1, keepdims=True)
        o = jnp.dot(p.astype(BF16), vw, preferred_element_type=F32) / l
        o_ref[pl.ds(pl.multiple_of(s * nq, nq), nq), :] = jnp.where(lo, o[:nq], o[nq:]).astype(BF16)
        return carry

    lax.fori_loop(0, steps, body, 0, unroll=NA_UNROLL)


def _neighbourhood(proj, tables, batch, seq):
    m, _ = proj.shape
    rows = seq // GRID_W
    starts, types, _, win = _na_step_geometry(rows)
    pairs = NA_HEADS // 2
    tok = NA_HEADS * NA_HEAD_DIM
    n_types, nq2, nk = tables.shape[1:]
    hblk = lambda off: pl.BlockSpec((seq, LANES), lambda b, p: (b, off + p))
    smem = pl.BlockSpec(memory_space=pltpu.SMEM)
    return pl.pallas_call(
        functools.partial(_na_kernel, win=win),
        grid=(batch, pairs),
        in_specs=[smem, smem, hblk(0), hblk(pairs), hblk(2 * pairs),
                  pl.BlockSpec((None, n_types, nq2, nk), lambda b, p: (p, 0, 0, 0))],
        out_specs=pl.BlockSpec((seq, LANES), lambda b, p: (b, p)),
        out_shape=jax.ShapeDtypeStruct((m, tok), BF16),
        compiler_params=pltpu.CompilerParams(dimension_semantics=("arbitrary", "arbitrary"),
                                             vmem_limit_bytes=VMEM_LIMIT),
        name="neighbourhood",
    )(jnp.asarray(starts, jnp.int32), jnp.asarray(types, jnp.int32), proj, proj, proj, tables)


def _post_kernel(y_ref, mq_ref, mk_ref, mv_ref, h_ref, wo_ref, g_ref, w1_ref, w2_ref, o_ref, f_ref):
    tok = y_ref.shape[1]
    lane = lax.broadcasted_iota(jnp.int32, (1, MEM_WIDTH), 1)
    mq = mq_ref[...] * (MEM_HEAD_DIM ** -0.5)
    mk = mk_ref[...]
    mv = mv_ref[...]
    m = jnp.zeros(mq.shape, F32)
    for hd in range(MEM_HEADS):
        sel = (lane >= hd * MEM_HEAD_DIM) & (lane < (hd + 1) * MEM_HEAD_DIM)
        qh = jnp.where(sel, mq, jnp.zeros_like(mq))
        sc = lax.dot_general(qh, mk, _NT, preferred_element_type=F32)
        sc = sc - jnp.max(sc, axis=-1, keepdims=True)
        p = jnp.exp(sc)
        l = jnp.sum(p, axis=-1, keepdims=True)
        vh = jnp.where(sel, mv, jnp.zeros_like(mv))
        m += jnp.dot(p.astype(BF16), vh, preferred_element_type=F32) / l

    y = jnp.dot(y_ref[...], wo_ref[:tok, :], preferred_element_type=F32)
    y += jnp.dot(m.astype(BF16), wo_ref[tok:, :], preferred_element_type=F32)
    h1 = h_ref[...] + _rms(y, g_ref[0:1, :])
    a = _rms(h1, g_ref[1:2, :]).astype(BF16)

    f_ref[...] = jnp.zeros_like(f_ref)

    def ffn_body(c, carry):
        gate = jnp.dot(a, w1_ref[0, c], preferred_element_type=F32)
        up = jnp.dot(a, w1_ref[1, c], preferred_element_type=F32)
        act = (gate * jax.nn.sigmoid(gate) * up).astype(BF16)
        f_ref[...] += jnp.dot(act, w2_ref[c], preferred_element_type=F32)
        return carry

    lax.fori_loop(0, w2_ref.shape[0], ffn_body, 0, unroll=True)
    o_ref[...] = h1 + _rms(f_ref[...], g_ref[2:3, :])


def _post(y, proj, mq_block, mem_k, mem_v, h, w_out, gains, w1, w2, tm, seq, mem_len):
    m, d = h.shape
    tok = y.shape[1]
    tiles_per_seq = seq // tm
    n_chunks, hc = w2.shape[0], w2.shape[1]
    return pl.pallas_call(
        _post_kernel,
        grid=(m // tm,),
        in_specs=[pl.BlockSpec((tm, tok), lambda i: (i, 0)),
                  pl.BlockSpec((tm, MEM_WIDTH), lambda i: (i, mq_block)),
                  pl.BlockSpec((mem_len, MEM_WIDTH), lambda i: (i // tiles_per_seq, 0)),
                  pl.BlockSpec((mem_len, MEM_WIDTH), lambda i: (i // tiles_per_seq, 0)),
                  pl.BlockSpec((tm, d), lambda i: (i, 0)),
                  _const_spec((d, d)),
                  _const_spec((3, d)),
                  _const_spec((2, n_chunks, d, hc)),
                  _const_spec((n_chunks, hc, d))],
        out_specs=pl.BlockSpec((tm, d), lambda i: (i, 0)),
        out_shape=jax.ShapeDtypeStruct((m, d), F32),
        scratch_shapes=[pltpu.VMEM((tm, d), F32)],
        compiler_params=pltpu.CompilerParams(dimension_semantics=("arbitrary",),
                                             vmem_limit_bytes=VMEM_LIMIT),
        name="post",
    )(y, proj, mem_k, mem_v, h, w_out, gains, w1, w2)


FFN_CHUNK = 256


def kernel(x, mem, norm_g, mem_norm_g, mem_w_kv, ret_w_in, ret_w_out, ret_gn_g, ret_gn_b, ret_decay,
           na_w_in, na_w_out, na_rpb, ffn_w_in, ffn_w_out):
    batch, seq, d = x.shape
    mem_len = mem.shape[1]
    depth = norm_g.shape[0]
    hidden = ffn_w_out.shape[1]
    tm = min(512, seq)
    rows = seq // GRID_W

    mem_k, mem_v = _mem_kv(mem.reshape(batch * mem_len, d), mem_norm_g, mem_w_kv.astype(BF16), mem_len)
    rot = _rotary_tables(seq)
    h = x.reshape(batch * seq, d)
    for i in range(depth):
        j = i // 2
        if i % 2 == 0:
            proj = _in_proj(h, norm_g[i, 0], ret_w_in[j].astype(BF16), rot, tm, seq)
            log_gamma = -jnp.exp(ret_decay[j].astype(F32))
            y = _retention(proj, log_gamma, ret_gn_g[j], ret_gn_b[j], batch, seq)
            w_out = ret_w_out[j]
        else:
            proj = _in_proj(h, norm_g[i, 0], na_w_in[j].astype(BF16), None, tm, seq)
            y = _neighbourhood(proj, _na_bias_tables(na_rpb[j], rows), batch, seq)
            w_out = na_w_out[j]
        mq_block = (proj.shape[1] - MEM_WIDTH) // MEM_WIDTH
        n_chunks = hidden // FFN_CHUNK
        w1 = ffn_w_in[i].astype(BF16).reshape(d, 2, n_chunks, FFN_CHUNK).transpose(1, 2, 0, 3)
        w2 = ffn_w_out[i].astype(BF16).reshape(n_chunks, FFN_CHUNK, d)
        h = _post(y, proj, mq_block, mem_k, mem_v, h, w_out.astype(BF16), norm_g[i, 1:4], w1, w2,
                  tm, seq, mem_len)
    return h.reshape(batch, seq, d)
```

```python
import functools

import numpy as np
import jax
import jax.numpy as jnp
from jax import lax
from jax.experimental import pallas as pl
from jax.experimental.pallas import tpu as pltpu

F32 = jnp.float32
BF16 = jnp.bfloat16

LANES = 128
MEM_HEADS = 4
MEM_HEAD_DIM = 64
MEM_WIDTH = MEM_HEADS * MEM_HEAD_DIM
RET_HEADS = 6
RET_HEAD_DIM = 128
RET_CHUNK = 128
RET_ROPE_BASE = 10000.0
RET_UNROLL = 8
GN_EPS = 1e-5
NA_HEADS = 12
NA_HEAD_DIM = 64
GRID_W = 64
NA_WIN_R = 8
NA_WIN_C = 16
NA_ROWS_PER_STEP = 2
NA_STEPS_PER_ITER = 8
FFN_CHUNK = 256
EPS = 1e-6
NEG = -1e30
LOG2E = 1.4426950408889634
VMEM_LIMIT = 56 * 1024 * 1024

_NT = (((1,), (1,)), ((), ()))


def _rms(x, g):
    return x * lax.rsqrt(jnp.mean(x * x, axis=-1, keepdims=True) + EPS) * g


def _const_spec(shape):
    nd = len(shape)
    return pl.BlockSpec(shape, lambda *_: (0,) * nd, pipeline_mode=pl.Buffered(1))


def _mem_kv_kernel(mem_ref, g_ref, w_ref, k_ref, v_ref):
    a = _rms(mem_ref[...], g_ref[...]).astype(BF16)
    kv = jnp.dot(a, w_ref[...], preferred_element_type=F32)
    k_ref[...] = kv[:, :MEM_WIDTH].astype(BF16)
    v_ref[...] = kv[:, MEM_WIDTH:].astype(BF16)


def _mem_kv(mem2d, g, w_kv, mem_len):
    m, d = mem2d.shape
    return pl.pallas_call(
        _mem_kv_kernel,
        grid=(m // mem_len,),
        in_specs=[pl.BlockSpec((mem_len, d), lambda i: (i, 0)),
                  _const_spec((1, d)),
                  _const_spec((d, 2 * MEM_WIDTH))],
        out_specs=[pl.BlockSpec((mem_len, MEM_WIDTH), lambda i: (i, 0))] * 2,
        out_shape=[jax.ShapeDtypeStruct((m, MEM_WIDTH), BF16)] * 2,
        name="mem_kv",
    )(mem2d, g.reshape(1, d), w_kv)


def _in_proj_kernel(h_ref, g_ref, w_ref, *rest, n_rot):
    if n_rot:
        cos_ref, sin_ref, o_ref = rest
    else:
        (o_ref,) = rest
    a = _rms(h_ref[...], g_ref[...]).astype(BF16)
    n = o_ref.shape[1]
    step = 2 * LANES
    for c in range(n // step):
        y = jnp.dot(a, w_ref[:, c * step:(c + 1) * step], preferred_element_type=F32)
        if 2 * c < n_rot:
            halves = []
            for y1 in (y[:, :LANES], y[:, LANES:]):
                halves.append(y1 * cos_ref[...] + pltpu.roll(y1, LANES // 2, 1) * sin_ref[...])
            y = jnp.concatenate(halves, axis=1)
        o_ref[:, c * step:(c + 1) * step] = y.astype(BF16)


def _in_proj(h, g, w, rot, tm, seq):
    m, d = h.shape
    n = w.shape[1]
    in_specs = [pl.BlockSpec((tm, d), lambda i: (i, 0)), _const_spec((1, d)), _const_spec((d, n))]
    args = [h, g.reshape(1, d), w]
    n_rot = 0
    if rot is not None:
        n_rot = 2 * RET_HEADS
        tiles_per_seq = seq // tm
        in_specs += [pl.BlockSpec((tm, LANES), lambda i: (i % tiles_per_seq, 0))] * 2
        args += list(rot)
    return pl.pallas_call(
        functools.partial(_in_proj_kernel, n_rot=n_rot),
        grid=(m // tm,),
        in_specs=in_specs,
        out_specs=pl.BlockSpec((tm, n), lambda i: (i, 0)),
        out_shape=jax.ShapeDtypeStruct((m, n), BF16),
        compiler_params=pltpu.CompilerParams(dimension_semantics=("arbitrary",),
                                             vmem_limit_bytes=VMEM_LIMIT),
        name="in_proj",
    )(*args)


def _rotary_tables(seq):
    half = RET_HEAD_DIM // 2
    inv = RET_ROPE_BASE ** (-jnp.arange(half, dtype=F32) / half)
    ang = jnp.arange(seq, dtype=F32)[:, None] * inv[None, :]
    c, s = jnp.cos(ang), jnp.sin(ang)
    return jnp.concatenate([c, c], axis=1), jnp.concatenate([-s, s], axis=1)


def _ret_kernel(lg_ref, q_ref, k_ref, v_ref, g_ref, gng_ref, gnb_ref, o_ref, p_ref, y_ref, kv_ref, st_ref):
    hd = pl.program_id(1)
    c = RET_CHUNK
    dk = RET_HEAD_DIM
    n_chunks = q_ref.shape[0] // c
    lg_f = lg_ref[0, hd]
    lg_b = lg_ref[1, hd]
    scale = RET_HEAD_DIM ** -0.5

    row = lax.broadcasted_iota(jnp.int32, (c, c), 0)
    col = lax.broadcasted_iota(jnp.int32, (c, c), 1)
    rel = (row - col).astype(F32)
    decay = (jnp.where(rel >= 0, jnp.exp(lg_f * jnp.maximum(rel, 0.0)), 0.0)
             + jnp.where(rel <= 0, jnp.exp(lg_b * jnp.maximum(-rel, 0.0)), 0.0)) * scale
    tok_row = row.astype(F32)
    tok_col = col.astype(F32)
    cd_f = jnp.exp(jnp.full((1, 1), c, F32) * lg_f)
    cd_b = jnp.exp(jnp.full((1, 1), c, F32) * lg_b)

    def chunk(n):
        return pl.ds(pl.multiple_of(n * c, c), c)

    kdec_f = jnp.exp(lg_f * (c - 1 - tok_col))
    kdec_b = jnp.exp(lg_b * tok_col)

    def kv_body(n, carry):
        kt = k_ref[chunk(n), :].astype(F32).T
        kd = jnp.concatenate([(kt * kdec_f).astype(BF16), (kt * kdec_b).astype(BF16)], axis=0)
        kv_ref[n] = jnp.dot(kd, v_ref[chunk(n), :], preferred_element_type=F32)
        s = jnp.dot(q_ref[chunk(n), :], kt.astype(BF16), preferred_element_type=F32) * decay
        p_ref[chunk(n), :] = s.astype(BF16)
        return carry

    lax.fori_loop(0, n_chunks, kv_body, 0, unroll=RET_UNROLL)

    def scan_body(i, states):
        s_f, s_b = states
        j = n_chunks - 1 - i
        st_ref[i, :dk, :] = s_f.astype(BF16)
        st_ref[j, dk:, :] = s_b.astype(BF16)
        return s_f * cd_f + kv_ref[i, :dk, :], s_b * cd_b + kv_ref[j, dk:, :]

    zero = jnp.zeros((dk, RET_HEAD_DIM), F32)
    lax.fori_loop(0, n_chunks, scan_body, (zero, zero), unroll=RET_UNROLL)

    qdec_f = jnp.exp(lg_f * (tok_row + 1)) * scale
    qdec_b = jnp.exp(lg_b * (c - tok_row)) * scale

    def y_body(n, carry):
        qf32 = q_ref[chunk(n), :].astype(F32)
        qd = jnp.concatenate([(qf32 * qdec_f).astype(BF16), (qf32 * qdec_b).astype(BF16)], axis=1)
        y_ref[chunk(n), :] = (jnp.dot(p_ref[chunk(n), :], v_ref[chunk(n), :], preferred_element_type=F32)
                              + jnp.dot(qd, st_ref[n], preferred_element_type=F32))
        return carry

    lax.fori_loop(0, n_chunks, y_body, 0, unroll=RET_UNROLL)

    def norm_body(n, carry):
        y = y_ref[chunk(n), :]
        mu = jnp.mean(y, axis=-1, keepdims=True)
        var = jnp.maximum(jnp.mean(y * y, axis=-1, keepdims=True) - mu * mu, 0.0)
        yn = (y - mu) * lax.rsqrt(var + GN_EPS) * gng_ref[...] + gnb_ref[...]
        gate = g_ref[chunk(n), :].astype(F32)
        o_ref[chunk(n), :] = (gate * jax.nn.sigmoid(gate) * yn).astype(BF16)
        return carry

    lax.fori_loop(0, n_chunks, norm_body, 0, unroll=RET_UNROLL)


def _retention(proj, log_gamma, gn_g, gn_b, batch, seq):
    m, _ = proj.shape
    n_chunks = seq // RET_CHUNK
    hblk = lambda off: pl.BlockSpec((seq, RET_HEAD_DIM), lambda b, h: (b, off + h))
    vec = pl.BlockSpec((1, RET_HEAD_DIM), lambda b, h: (0, h))
    tok = RET_HEADS * RET_HEAD_DIM
    return pl.pallas_call(
        _ret_kernel,
        grid=(batch, RET_HEADS),
        in_specs=[pl.BlockSpec(memory_space=pltpu.SMEM),
                  hblk(0), hblk(RET_HEADS), hblk(2 * RET_HEADS), hblk(3 * RET_HEADS), vec, vec],
        out_specs=pl.BlockSpec((seq, RET_HEAD_DIM), lambda b, h: (b, h)),
        out_shape=jax.ShapeDtypeStruct((m, tok), BF16),
        scratch_shapes=[pltpu.VMEM((seq, RET_CHUNK), BF16),
                        pltpu.VMEM((seq, RET_HEAD_DIM), F32),
                        pltpu.VMEM((n_chunks, 2 * RET_HEAD_DIM, RET_HEAD_DIM), F32),
                        pltpu.VMEM((n_chunks, 2 * RET_HEAD_DIM, RET_HEAD_DIM), BF16)],
        compiler_params=pltpu.CompilerParams(dimension_semantics=("arbitrary", "arbitrary"),
                                             vmem_limit_bytes=VMEM_LIMIT),
        name="retention",
    )(log_gamma, proj, proj, proj, proj, gn_g.reshape(1, tok), gn_b.reshape(1, tok))


def _na_step_geometry(rows):
    rg = NA_ROWS_PER_STEP
    wr = min(NA_WIN_R, rows)
    win = wr + rg - 1
    steps = rows // rg
    starts = [int(np.clip(np.clip(s * rg - wr // 2, 0, rows - wr), 0, rows - win)) for s in range(steps)]
    reps, types = [], []
    for s in range(steps):
        sig = (starts[s] - s * rg,
               tuple(int(np.clip(s * rg + j - wr // 2, 0, rows - wr)) - s * rg for j in range(rg)))
        for t, (sig_t, _) in enumerate(reps):
            if sig_t == sig:
                types.append(t)
                break
        else:
            types.append(len(reps))
            reps.append((sig, s))
    return starts, types, [s for _, s in reps], win


def _na_bias_tables(rpb, rows):
    rg = NA_ROWS_PER_STEP
    wr = min(NA_WIN_R, rows)
    starts, _, rep_steps, win = _na_step_geometry(rows)
    heads, n_dr, n_dc = rpb.shape
    w = GRID_W
    dc = np.arange(w)[None, :] - np.arange(w)[:, None] + NA_WIN_C - 1
    onehot = (dc[None] == np.arange(n_dc)[:, None, None]).astype(np.float32)
    band = jnp.einsum("hrd,dqk->hrqk", rpb.astype(F32), jnp.asarray(onehot),
                      precision=lax.Precision.HIGHEST)
    pad = rg + wr
    band = jnp.pad(band, ((0, 0), (pad, pad), (0, 0), (0, 0)))
    parts, ok_all = [], []
    for s in rep_steps:
        per_row = []
        for j in range(rg):
            lo = starts[s] - (s * rg + j) + NA_WIN_R - 1 + pad
            per_row.append(band[:, lo:lo + win])
        parts.append(jnp.stack(per_row, axis=1))
        qrow = (s * rg + np.arange(rg))[:, None, None, None]
        qcol = np.arange(w)[None, :, None, None]
        krow = (starts[s] + np.arange(win))[None, None, :, None]
        kcol = np.arange(w)[None, None, None, :]
        rs = np.clip(qrow - wr // 2, 0, rows - wr)
        cs = np.clip(qcol - NA_WIN_C // 2, 0, w - NA_WIN_C)
        ok = (krow >= rs) & (krow < rs + wr) & (kcol >= cs) & (kcol < cs + NA_WIN_C)
        ok_all.append(ok.reshape(rg * w, win * w))
    tab = jnp.stack(parts, axis=1)
    n_types = len(rep_steps)
    tab = tab.transpose(0, 1, 2, 4, 3, 5).reshape(heads, n_types, rg * w, win * w)
    tab = jnp.where(jnp.asarray(np.stack(ok_all))[None], tab * LOG2E, NEG)
    tab = tab.reshape(heads // 2, 2, n_types, rg * w, win * w).transpose(0, 2, 1, 3, 4)
    return tab.reshape(heads // 2, n_types, 2 * rg * w, win * w)


def _na_kernel(start_ref, type_ref, q_ref, k_ref, v_ref, tab_ref, o_ref, p_ref, r_ref, *, win):
    nq = NA_ROWS_PER_STEP * GRID_W
    nk = win * GRID_W
    steps = q_ref.shape[0] // nq
    lo = lax.broadcasted_iota(jnp.int32, (1, LANES), 1) < NA_HEAD_DIM
    scale = NA_HEAD_DIM ** -0.5 * LOG2E

    def rows(s):
        return pl.ds(pl.multiple_of(s * nq, nq), nq)

    def window(s):
        return pl.ds(pl.multiple_of(start_ref[s] * GRID_W, GRID_W), nk)

    def softmax(s, slot):
        q = (q_ref[rows(s), :].astype(F32) * scale).astype(BF16)
        zero = jnp.zeros_like(q)
        qh = jnp.concatenate([jnp.where(lo, q, zero), jnp.where(lo, zero, q)], axis=0)
        sc = lax.dot_general(qh, k_ref[window(s), :], _NT, preferred_element_type=F32)
        sc = sc + tab_ref[type_ref[s]]
        p = jnp.exp2(sc - jnp.max(sc, axis=-1, keepdims=True))
        p_ref[slot] = p.astype(BF16)
        r_ref[slot] = jnp.broadcast_to(1.0 / jnp.sum(p, axis=-1, keepdims=True), (2 * nq, LANES))

    def attend(s, slot):
        o = jnp.dot(p_ref[slot], v_ref[window(s), :], preferred_element_type=F32) * r_ref[slot]
        o_ref[rows(s), :] = jnp.where(lo, o[:nq], o[nq:]).astype(BF16)

    per_iter = NA_STEPS_PER_ITER
    softmax(0, 0)

    def run(s0, count, last):
        for j in range(count):
            if not (last and j == count - 1):
                softmax(s0 + j + 1, (j + 1) % 2)
            attend(s0 + j, j % 2)

    def body(i, carry):
        run(i * per_iter, per_iter, False)
        return carry

    lax.fori_loop(0, steps // per_iter - 1, body, 0)
    run(steps - per_iter, per_iter, True)


def _neighbourhood(proj, tables, batch, seq):
    m, _ = proj.shape
    rows = seq // GRID_W
    starts, types, _, win = _na_step_geometry(rows)
    pairs = NA_HEADS // 2
    tok = NA_HEADS * NA_HEAD_DIM
    n_types, nq2, nk = tables.shape[1:]
    hblk = lambda off: pl.BlockSpec((seq, LANES), lambda b, p: (b, off + p))
    smem = pl.BlockSpec(memory_space=pltpu.SMEM)
    return pl.pallas_call(
        functools.partial(_na_kernel, win=win),
        grid=(batch, pairs),
        in_specs=[smem, smem, hblk(0), hblk(pairs), hblk(2 * pairs),
                  pl.BlockSpec((None, n_types, nq2, nk), lambda b, p: (p, 0, 0, 0))],
        out_specs=pl.BlockSpec((seq, LANES), lambda b, p: (b, p)),
        out_shape=jax.ShapeDtypeStruct((m, tok), BF16),
        scratch_shapes=[pltpu.VMEM((2, nq2, nk), BF16), pltpu.VMEM((2, nq2, LANES), F32)],
        compiler_params=pltpu.CompilerParams(dimension_semantics=("arbitrary", "arbitrary"),
                                             vmem_limit_bytes=VMEM_LIMIT),
        name="neighbourhood",
    )(jnp.asarray(starts, jnp.int32), jnp.asarray(types, jnp.int32), proj, proj, proj, tables)


def _post_kernel(y_ref, mq_ref, mk_ref, mv_ref, h_ref, wo_ref, g_ref, w1_ref, w2_ref, o_ref, f_ref):
    tok = y_ref.shape[1]
    hidden = w2_ref.shape[0]
    lane = lax.broadcasted_iota(jnp.int32, (1, MEM_WIDTH), 1)
    mq = mq_ref[...] * (MEM_HEAD_DIM ** -0.5)
    mk = mk_ref[...]
    mv = mv_ref[...]
    m = jnp.zeros(mq.shape, F32)
    for hd in range(MEM_HEADS):
        sel = (lane >= hd * MEM_HEAD_DIM) & (lane < (hd + 1) * MEM_HEAD_DIM)
        qh = jnp.where(sel, mq, jnp.zeros_like(mq))
        sc = lax.dot_general(qh, mk, _NT, preferred_element_type=F32)
        sc = sc - jnp.max(sc, axis=-1, keepdims=True)
        p = jnp.exp(sc)
        l = jnp.sum(p, axis=-1, keepdims=True)
        vh = jnp.where(sel, mv, jnp.zeros_like(mv))
        m += jnp.dot(p.astype(BF16), vh, preferred_element_type=F32) / l

    y = jnp.dot(y_ref[...], wo_ref[:tok, :], preferred_element_type=F32)
    y += jnp.dot(m.astype(BF16), wo_ref[tok:, :], preferred_element_type=F32)
    h1 = h_ref[...] + _rms(y, g_ref[0:1, :])
    a = _rms(h1, g_ref[1:2, :]).astype(BF16)

    for c in range(hidden // FFN_CHUNK):
        cols = slice(c * FFN_CHUNK, (c + 1) * FFN_CHUNK)
        gate = jnp.dot(a, w1_ref[:, cols], preferred_element_type=F32)
        up = jnp.dot(a, w1_ref[:, hidden + c * FFN_CHUNK:hidden + (c + 1) * FFN_CHUNK],
                     preferred_element_type=F32)
        act = (gate * jax.nn.sigmoid(gate) * up).astype(BF16)
        part = jnp.dot(act, w2_ref[cols, :], preferred_element_type=F32)
        if c == 0:
            f_ref[...] = part
        else:
            f_ref[...] += part
    o_ref[...] = h1 + _rms(f_ref[...], g_ref[2:3, :])


def _post(y, proj, mq_block, mem_k, mem_v, h, w_out, gains, w1, w2, tm, seq, mem_len):
    m, d = h.shape
    tok = y.shape[1]
    tiles_per_seq = seq // tm
    hidden = w2.shape[0]
    assert hidden % FFN_CHUNK == 0 and w1.shape == (d, 2 * hidden)
    return pl.pallas_call(
        _post_kernel,
        grid=(m // tm,),
        in_specs=[pl.BlockSpec((tm, tok), lambda i: (i, 0)),
                  pl.BlockSpec((tm, MEM_WIDTH), lambda i: (i, mq_block)),
                  pl.BlockSpec((mem_len, MEM_WIDTH), lambda i: (i // tiles_per_seq, 0)),
                  pl.BlockSpec((mem_len, MEM_WIDTH), lambda i: (i // tiles_per_seq, 0)),
                  pl.BlockSpec((tm, d), lambda i: (i, 0)),
                  _const_spec((d, d)),
                  _const_spec((3, d)),
                  _const_spec((d, 2 * hidden)),
                  _const_spec((hidden, d))],
        out_specs=pl.BlockSpec((tm, d), lambda i: (i, 0)),
        out_shape=jax.ShapeDtypeStruct((m, d), F32),
        scratch_shapes=[pltpu.VMEM((tm, d), F32)],
        compiler_params=pltpu.CompilerParams(dimension_semantics=("arbitrary",),
                                             vmem_limit_bytes=VMEM_LIMIT),
        name="post",
    )(y, proj, mem_k, mem_v, h, w_out, gains, w1, w2)


def kernel(x, mem, norm_g, mem_norm_g, mem_w_kv, ret_w_in, ret_w_out, ret_gn_g, ret_gn_b, ret_decay,
           na_w_in, na_w_out, na_rpb, ffn_w_in, ffn_w_out):
    batch, seq, d = x.shape
    mem_len = mem.shape[1]
    depth = norm_g.shape[0]
    hidden = ffn_w_out.shape[1]
    tm = min(512, seq)
    rows = seq // GRID_W

    mem_k, mem_v = _mem_kv(mem.reshape(batch * mem_len, d), mem_norm_g, mem_w_kv.astype(BF16), mem_len)
    rot = _rotary_tables(seq)
    h = x.reshape(batch * seq, d)
    for i in range(depth):
        j = i // 2
        if i % 2 == 0:
            proj = _in_proj(h, norm_g[i, 0], ret_w_in[j].astype(BF16), rot, tm, seq)
            log_gamma = -jnp.exp(ret_decay[j].astype(F32))
            y = _retention(proj, log_gamma, ret_gn_g[j], ret_gn_b[j], batch, seq)
            w_out = ret_w_out[j]
        else:
            proj = _in_proj(h, norm_g[i, 0], na_w_in[j].astype(BF16), None, tm, seq)
            y = _neighbourhood(proj, _na_bias_tables(na_rpb[j], rows), batch, seq)
            w_out = na_w_out[j]
        mq_block = (proj.shape[1] - MEM_WIDTH) // MEM_WIDTH
        h = _post(y, proj, mq_block, mem_k, mem_v, h, w_out.astype(BF16), norm_g[i, 1:4],
                  ffn_w_in[i].astype(BF16), ffn_w_out[i].astype(BF16), tm, seq, mem_len)
    return h.reshape(batch, seq, d)
```

```python
import functools

import numpy as np
import jax
import jax.numpy as jnp
from jax import lax
from jax.experimental import pallas as pl
from jax.experimental.pallas import tpu as pltpu

F32 = jnp.float32
BF16 = jnp.bfloat16

LANES = 128
MEM_HEADS = 4
MEM_HEAD_DIM = 64
MEM_WIDTH = MEM_HEADS * MEM_HEAD_DIM
RET_HEADS = 6
RET_HEAD_DIM = 128
RET_CHUNK = 128
RET_ROPE_BASE = 10000.0
RET_UNROLL = 8
GN_EPS = 1e-5
NA_HEADS = 12
NA_HEAD_DIM = 64
GRID_W = 64
NA_WIN_R = 8
NA_WIN_C = 16
NA_ROWS_PER_STEP = 2
NA_STEPS_PER_ITER = 8
FFN_CHUNK = 256
EPS = 1e-6
NEG = -1e30
LOG2E = 1.4426950408889634
VMEM_LIMIT = 56 * 1024 * 1024

_NT = (((1,), (1,)), ((), ()))


def _rms(x, g):
    return x * lax.rsqrt(jnp.mean(x * x, axis=-1, keepdims=True) + EPS) * g


def _const_spec(shape):
    nd = len(shape)
    return pl.BlockSpec(shape, lambda *_: (0,) * nd, pipeline_mode=pl.Buffered(1))


def _layer_spec(stacked, layer):
    shape = stacked.shape[1:]
    nd = len(shape)
    return pl.BlockSpec((None,) + shape, lambda *_: (layer,) + (0,) * nd, pipeline_mode=pl.Buffered(1))


def _mem_kv_kernel(mem_ref, g_ref, w_ref, k_ref, v_ref):
    a = _rms(mem_ref[...], g_ref[...]).astype(BF16)
    kv = jnp.dot(a, w_ref[...], preferred_element_type=F32)
    k_ref[...] = kv[:, :MEM_WIDTH].astype(BF16)
    v_ref[...] = kv[:, MEM_WIDTH:].astype(BF16)


def _mem_kv(mem2d, g, w_kv, mem_len):
    m, d = mem2d.shape
    return pl.pallas_call(
        _mem_kv_kernel,
        grid=(m // mem_len,),
        in_specs=[pl.BlockSpec((mem_len, d), lambda i: (i, 0)),
                  _const_spec((1, d)),
                  _const_spec((d, 2 * MEM_WIDTH))],
        out_specs=[pl.BlockSpec((mem_len, MEM_WIDTH), lambda i: (i, 0))] * 2,
        out_shape=[jax.ShapeDtypeStruct((m, MEM_WIDTH), BF16)] * 2,
        name="mem_kv",
    )(mem2d, g.reshape(1, d), w_kv)


def _memory_attention(mq, mk, mv):
    lane = lax.broadcasted_iota(jnp.int32, (1, MEM_WIDTH), 1)
    mq = mq * (MEM_HEAD_DIM ** -0.5)
    m = jnp.zeros(mq.shape, F32)
    for hd in range(MEM_HEADS):
        sel = (lane >= hd * MEM_HEAD_DIM) & (lane < (hd + 1) * MEM_HEAD_DIM)
        qh = jnp.where(sel, mq, jnp.zeros_like(mq))
        sc = lax.dot_general(qh, mk, _NT, preferred_element_type=F32)
        sc = sc - jnp.max(sc, axis=-1, keepdims=True)
        p = jnp.exp(sc)
        l = jnp.sum(p, axis=-1, keepdims=True)
        vh = jnp.where(sel, mv, jnp.zeros_like(mv))
        m += jnp.dot(p.astype(BF16), vh, preferred_element_type=F32) / l
    return m


def _in_proj_kernel(h_ref, g_ref, w_ref, *rest, n_rot):
    if n_rot:
        cos_ref, sin_ref, o_ref = rest
    else:
        (o_ref,) = rest
    a = _rms(h_ref[...], g_ref[0:1, :]).astype(BF16)
    n = o_ref.shape[1]
    step = 2 * LANES
    for c in range(n // step):
        y = jnp.dot(a, w_ref[:, c * step:(c + 1) * step], preferred_element_type=F32)
        if 2 * c < n_rot:
            halves = []
            for y1 in (y[:, :LANES], y[:, LANES:]):
                halves.append(y1 * cos_ref[...] + pltpu.roll(y1, LANES // 2, 1) * sin_ref[...])
            y = jnp.concatenate(halves, axis=1)
        o_ref[:, c * step:(c + 1) * step] = y.astype(BF16)


def _in_proj(h, gains, layer, w, w_layer, rot, tm, seq):
    m, d = h.shape
    n = w.shape[2]
    in_specs = [pl.BlockSpec((tm, d), lambda i: (i, 0)), _layer_spec(gains, layer), _layer_spec(w, w_layer)]
    args = [h, gains, w]
    n_rot = 0
    if rot is not None:
        n_rot = 2 * RET_HEADS
        tiles_per_seq = seq // tm
        in_specs += [pl.BlockSpec((tm, LANES), lambda i: (i % tiles_per_seq, 0))] * 2
        args += list(rot)
    return pl.pallas_call(
        functools.partial(_in_proj_kernel, n_rot=n_rot),
        grid=(m // tm,),
        in_specs=in_specs,
        out_specs=pl.BlockSpec((tm, n), lambda i: (i, 0)),
        out_shape=jax.ShapeDtypeStruct((m, n), BF16),
        compiler_params=pltpu.CompilerParams(dimension_semantics=("arbitrary",),
                                             vmem_limit_bytes=VMEM_LIMIT),
        name="in_proj",
    )(*args)


def _rotary_tables(seq):
    half = RET_HEAD_DIM // 2
    inv = RET_ROPE_BASE ** (-jnp.arange(half, dtype=F32) / half)
    ang = jnp.arange(seq, dtype=F32)[:, None] * inv[None, :]
    c, s = jnp.cos(ang), jnp.sin(ang)
    return jnp.concatenate([c, c], axis=1), jnp.concatenate([-s, s], axis=1)


def _ret_kernel(lg_ref, q_ref, k_ref, v_ref, g_ref, gng_ref, gnb_ref, o_ref, p_ref, y_ref, kv_ref, st_ref):
    hd = pl.program_id(1)
    c = RET_CHUNK
    dk = RET_HEAD_DIM
    n_chunks = q_ref.shape[0] // c
    lg_f = lg_ref[0, hd]
    lg_b = lg_ref[1, hd]
    scale = RET_HEAD_DIM ** -0.5

    row = lax.broadcasted_iota(jnp.int32, (c, c), 0)
    col = lax.broadcasted_iota(jnp.int32, (c, c), 1)
    rel = (row - col).astype(F32)
    decay = (jnp.where(rel >= 0, jnp.exp(lg_f * jnp.maximum(rel, 0.0)), 0.0)
             + jnp.where(rel <= 0, jnp.exp(lg_b * jnp.maximum(-rel, 0.0)), 0.0)) * scale
    tok_row = row.astype(F32)
    tok_col = col.astype(F32)
    cd_f = jnp.exp(jnp.full((1, 1), c, F32) * lg_f)
    cd_b = jnp.exp(jnp.full((1, 1), c, F32) * lg_b)

    def chunk(n):
        return pl.ds(pl.multiple_of(n * c, c), c)

    kdec_f = jnp.exp(lg_f * (c - 1 - tok_col))
    kdec_b = jnp.exp(lg_b * tok_col)

    def kv_body(n, carry):
        kt = k_ref[chunk(n), :].astype(F32).T
        kd = jnp.concatenate([(kt * kdec_f).astype(BF16), (kt * kdec_b).astype(BF16)], axis=0)
        kv_ref[n] = jnp.dot(kd, v_ref[chunk(n), :], preferred_element_type=F32)
        s = jnp.dot(q_ref[chunk(n), :], kt.astype(BF16), preferred_element_type=F32) * decay
        p_ref[chunk(n), :] = s.astype(BF16)
        return carry

    lax.fori_loop(0, n_chunks, kv_body, 0, unroll=RET_UNROLL)

    def scan_body(i, states):
        s_f, s_b = states
        j = n_chunks - 1 - i
        st_ref[i, :dk, :] = s_f.astype(BF16)
        st_ref[j, dk:, :] = s_b.astype(BF16)
        return s_f * cd_f + kv_ref[i, :dk, :], s_b * cd_b + kv_ref[j, dk:, :]

    zero = jnp.zeros((dk, RET_HEAD_DIM), F32)
    lax.fori_loop(0, n_chunks, scan_body, (zero, zero), unroll=RET_UNROLL)

    qdec_f = jnp.exp(lg_f * (tok_row + 1)) * scale
    qdec_b = jnp.exp(lg_b * (c - tok_row)) * scale

    def y_body(n, carry):
        qf32 = q_ref[chunk(n), :].astype(F32)
        qd = jnp.concatenate([(qf32 * qdec_f).astype(BF16), (qf32 * qdec_b).astype(BF16)], axis=1)
        y_ref[chunk(n), :] = (jnp.dot(p_ref[chunk(n), :], v_ref[chunk(n), :], preferred_element_type=F32)
                              + jnp.dot(qd, st_ref[n], preferred_element_type=F32))
        return carry

    lax.fori_loop(0, n_chunks, y_body, 0, unroll=RET_UNROLL)

    def norm_body(n, carry):
        y = y_ref[chunk(n), :]
        mu = jnp.mean(y, axis=-1, keepdims=True)
        var = jnp.maximum(jnp.mean(y * y, axis=-1, keepdims=True) - mu * mu, 0.0)
        yn = (y - mu) * lax.rsqrt(var + GN_EPS) * gng_ref[...] + gnb_ref[...]
        gate = g_ref[chunk(n), :].astype(F32)
        o_ref[chunk(n), :] = (gate * jax.nn.sigmoid(gate) * yn).astype(BF16)
        return carry

    lax.fori_loop(0, n_chunks, norm_body, 0, unroll=RET_UNROLL)


def _retention(proj, log_gamma, gn_g, gn_b, batch, seq):
    m, _ = proj.shape
    n_chunks = seq // RET_CHUNK
    hblk = lambda off: pl.BlockSpec((seq, RET_HEAD_DIM), lambda b, h: (b, off + h))
    vec = pl.BlockSpec((1, RET_HEAD_DIM), lambda b, h: (0, h))
    tok = RET_HEADS * RET_HEAD_DIM
    return pl.pallas_call(
        _ret_kernel,
        grid=(batch, RET_HEADS),
        in_specs=[pl.BlockSpec(memory_space=pltpu.SMEM),
                  hblk(0), hblk(RET_HEADS), hblk(2 * RET_HEADS), hblk(3 * RET_HEADS), vec, vec],
        out_specs=pl.BlockSpec((seq, RET_HEAD_DIM), lambda b, h: (b, h)),
        out_shape=jax.ShapeDtypeStruct((m, tok), BF16),
        scratch_shapes=[pltpu.VMEM((seq, RET_CHUNK), BF16),
                        pltpu.VMEM((seq, RET_HEAD_DIM), F32),
                        pltpu.VMEM((n_chunks, 2 * RET_HEAD_DIM, RET_HEAD_DIM), F32),
                        pltpu.VMEM((n_chunks, 2 * RET_HEAD_DIM, RET_HEAD_DIM), BF16)],
        compiler_params=pltpu.CompilerParams(dimension_semantics=("arbitrary", "arbitrary"),
                                             vmem_limit_bytes=VMEM_LIMIT),
        name="retention",
    )(log_gamma, proj, proj, proj, proj, gn_g.reshape(1, tok), gn_b.reshape(1, tok))


def _na_step_geometry(rows):
    rg = NA_ROWS_PER_STEP
    wr = min(NA_WIN_R, rows)
    win = wr + rg - 1
    steps = rows // rg
    starts = [int(np.clip(np.clip(s * rg - wr // 2, 0, rows - wr), 0, rows - win)) for s in range(steps)]
    reps, types = [], []
    for s in range(steps):
        sig = (starts[s] - s * rg,
               tuple(int(np.clip(s * rg + j - wr // 2, 0, rows - wr)) - s * rg for j in range(rg)))
        for t, (sig_t, _) in enumerate(reps):
            if sig_t == sig:
                types.append(t)
                break
        else:
            types.append(len(reps))
            reps.append((sig, s))
    return starts, types, [s for _, s in reps], win


def _na_bias_tables(rpb, rows):
    rg = NA_ROWS_PER_STEP
    wr = min(NA_WIN_R, rows)
    starts, _, rep_steps, win = _na_step_geometry(rows)
    heads, n_dr, n_dc = rpb.shape
    assert 2 * GRID_W == LANES and n_dc <= GRID_W
    pad = rg + wr
    lo = np.zeros((len(rep_steps), rg), np.int32)
    row_ok = np.zeros((len(rep_steps), rg * win), np.int32)
    for t, s in enumerate(rep_steps):
        for j in range(rg):
            qrow = s * rg + j
            lo[t, j] = starts[s] - qrow + NA_WIN_R - 1 + pad
            rs = int(np.clip(qrow - wr // 2, 0, rows - wr))
            krow = starts[s] + np.arange(win)
            row_ok[t, j * win:(j + 1) * win] = (krow >= rs) & (krow < rs + wr)
    rpb_rows = jnp.pad(rpb.astype(F32), ((0, 0), (pad, pad), (0, LANES - n_dc)))
    smem = pl.BlockSpec(memory_space=pltpu.SMEM)
    return pl.pallas_call(
        functools.partial(_na_table_kernel, win=win),
        grid=(heads // 2, len(rep_steps)),
        in_specs=[smem, smem, pl.BlockSpec((2, n_dr + 2 * pad, LANES), lambda p, t: (p, 0, 0))],
        out_specs=pl.BlockSpec((None, None, 2 * rg * GRID_W, win * GRID_W), lambda p, t: (p, t, 0, 0)),
        out_shape=jax.ShapeDtypeStruct((heads // 2, len(rep_steps), 2 * rg * GRID_W, win * GRID_W), F32),
        name="na_tables",
    )(jnp.asarray(lo), jnp.asarray(row_ok), rpb_rows)


def _na_table_kernel(lo_ref, ok_ref, r_ref, o_ref, *, win):
    t = pl.program_id(1)
    rg = NA_ROWS_PER_STEP
    shape = (GRID_W, LANES)
    qc = lax.broadcasted_iota(jnp.int32, shape, 0)
    lane = lax.broadcasted_iota(jnp.int32, shape, 1)
    kc = lane & (GRID_W - 1)
    cs = jnp.clip(qc - NA_WIN_C // 2, 0, GRID_W - NA_WIN_C)
    col_ok = (kc >= cs) & (kc < cs + NA_WIN_C)
    for hd in range(2):
        for j in range(rg):
            base = lo_ref[t, j]
            rows = slice((hd * rg + j) * GRID_W, (hd * rg + j + 1) * GRID_W)
            for w0 in range(0, win, 2):
                pair = w0 + 1 < win
                x = r_ref[hd, pl.ds(base + w0, 1), :]
                ok_lo = ok_ref[t, j * win + w0]
                ok_hi = ok_ref[t, j * win + w0 + 1] if pair else ok_lo
                if pair:
                    x = x + pltpu.roll(r_ref[hd, pl.ds(base + w0 + 1, 1), :], GRID_W, 1)
                row_ok = jnp.where(lane < GRID_W, ok_lo, ok_hi) != 0
                band = pltpu.roll(jnp.broadcast_to(x, shape), LANES - (NA_WIN_C - 1), 1,
                                  stride=1, stride_axis=0)
                val = jnp.where(col_ok & row_ok, band * LOG2E, NEG)
                if pair:
                    o_ref[rows, w0 * GRID_W:(w0 + 2) * GRID_W] = val
                else:
                    o_ref[rows, w0 * GRID_W:(w0 + 1) * GRID_W] = val[:, :GRID_W]


def _na_kernel(start_ref, type_ref, q_ref, k_ref, v_ref, tab_ref, o_ref, p_ref, r_ref, *, win):
    nq = NA_ROWS_PER_STEP * GRID_W
    nk = win * GRID_W
    steps = q_ref.shape[0] // nq
    lo = lax.broadcasted_iota(jnp.int32, (1, LANES), 1) < NA_HEAD_DIM
    scale = NA_HEAD_DIM ** -0.5 * LOG2E

    def rows(s):
        return pl.ds(pl.multiple_of(s * nq, nq), nq)

    def window(s):
        return pl.ds(pl.multiple_of(start_ref[s] * GRID_W, GRID_W), nk)

    def softmax(s, slot):
        q = (q_ref[rows(s), :].astype(F32) * scale).astype(BF16)
        zero = jnp.zeros_like(q)
        qh = jnp.concatenate([jnp.where(lo, q, zero), jnp.where(lo, zero, q)], axis=0)
        sc = lax.dot_general(qh, k_ref[window(s), :], _NT, preferred_element_type=F32)
        sc = sc + tab_ref[type_ref[s]]
        p = jnp.exp2(sc - jnp.max(sc, axis=-1, keepdims=True))
        p_ref[slot] = p.astype(BF16)
        r_ref[slot] = jnp.broadcast_to(1.0 / jnp.sum(p, axis=-1, keepdims=True), (2 * nq, LANES))

    def attend(s, slot):
        o = jnp.dot(p_ref[slot], v_ref[window(s), :], preferred_element_type=F32) * r_ref[slot]
        o_ref[rows(s), :] = jnp.where(lo, o[:nq], o[nq:]).astype(BF16)

    per_iter = NA_STEPS_PER_ITER
    softmax(0, 0)

    def run(s0, count, last):
        for j in range(count):
            if not (last and j == count - 1):
                softmax(s0 + j + 1, (j + 1) % 2)
            attend(s0 + j, j % 2)

    def body(i, carry):
        run(i * per_iter, per_iter, False)
        return carry

    lax.fori_loop(0, steps // per_iter - 1, body, 0)
    run(steps - per_iter, per_iter, True)


def _neighbourhood(proj, tables, batch, seq):
    m, _ = proj.shape
    rows = seq // GRID_W
    starts, types, _, win = _na_step_geometry(rows)
    pairs = NA_HEADS // 2
    tok = NA_HEADS * NA_HEAD_DIM
    n_types, nq2, nk = tables.shape[1:]
    hblk = lambda off: pl.BlockSpec((seq, LANES), lambda b, p: (b, off + p))
    smem = pl.BlockSpec(memory_space=pltpu.SMEM)
    return pl.pallas_call(
        functools.partial(_na_kernel, win=win),
        grid=(batch, pairs),
        in_specs=[smem, smem, hblk(0), hblk(pairs), hblk(2 * pairs),
                  pl.BlockSpec((None, n_types, nq2, nk), lambda b, p: (p, 0, 0, 0))],
        out_specs=pl.BlockSpec((seq, LANES), lambda b, p: (b, p)),
        out_shape=jax.ShapeDtypeStruct((m, tok), BF16),
        scratch_shapes=[pltpu.VMEM((2, nq2, nk), BF16), pltpu.VMEM((2, nq2, LANES), F32)],
        compiler_params=pltpu.CompilerParams(dimension_semantics=("arbitrary", "arbitrary"),
                                             vmem_limit_bytes=VMEM_LIMIT),
        name="neighbourhood",
    )(jnp.asarray(starts, jnp.int32), jnp.asarray(types, jnp.int32), proj, proj, proj, tables)


def _post_kernel(y_ref, mq_ref, mk_ref, mv_ref, h_ref, wo_ref, g_ref, w1_ref, w2_ref, o_ref, f_ref):
    tok = y_ref.shape[1]
    hidden = w2_ref.shape[0]
    m = _memory_attention(mq_ref[...], mk_ref[...], mv_ref[...])
    y = jnp.dot(y_ref[...], wo_ref[:tok, :], preferred_element_type=F32)
    y += jnp.dot(m.astype(BF16), wo_ref[tok:, :], preferred_element_type=F32)
    h1 = h_ref[...] + _rms(y, g_ref[1:2, :])
    a = _rms(h1, g_ref[2:3, :]).astype(BF16)

    for c in range(hidden // FFN_CHUNK):
        cols = slice(c * FFN_CHUNK, (c + 1) * FFN_CHUNK)
        gate = jnp.dot(a, w1_ref[:, cols], preferred_element_type=F32)
        up = jnp.dot(a, w1_ref[:, hidden + c * FFN_CHUNK:hidden + (c + 1) * FFN_CHUNK],
                     preferred_element_type=F32)
        act = (gate * jax.nn.sigmoid(gate) * up).astype(BF16)
        part = jnp.dot(act, w2_ref[cols, :], preferred_element_type=F32)
        if c == 0:
            f_ref[...] = part
        else:
            f_ref[...] += part
    o_ref[...] = h1 + _rms(f_ref[...], g_ref[3:4, :])


def _post(y, proj, mem_k, mem_v, h, w_out, w_out_layer, gains, w1, w2, layer, tm, seq, mem_len):
    m, d = h.shape
    tok = y.shape[1]
    mq_block = (proj.shape[1] - MEM_WIDTH) // MEM_WIDTH
    tiles_per_seq = seq // tm
    hidden = w2.shape[1]
    assert hidden % FFN_CHUNK == 0 and w1.shape[1:] == (d, 2 * hidden)
    return pl.pallas_call(
        _post_kernel,
        grid=(m // tm,),
        in_specs=[pl.BlockSpec((tm, tok), lambda i: (i, 0)),
                  pl.BlockSpec((tm, MEM_WIDTH), lambda i: (i, mq_block)),
                  pl.BlockSpec((mem_len, MEM_WIDTH), lambda i: (i // tiles_per_seq, 0)),
                  pl.BlockSpec((mem_len, MEM_WIDTH), lambda i: (i // tiles_per_seq, 0)),
                  pl.BlockSpec((tm, d), lambda i: (i, 0)),
                  _layer_spec(w_out, w_out_layer),
                  _layer_spec(gains, layer),
                  _layer_spec(w1, layer),
                  _layer_spec(w2, layer)],
        out_specs=pl.BlockSpec((tm, d), lambda i: (i, 0)),
        out_shape=jax.ShapeDtypeStruct((m, d), F32),
        scratch_shapes=[pltpu.VMEM((tm, d), F32)],
        compiler_params=pltpu.CompilerParams(dimension_semantics=("arbitrary",),
                                             vmem_limit_bytes=VMEM_LIMIT),
        name="post",
    )(y, proj, mem_k, mem_v, h, w_out, gains, w1, w2)


def kernel(x, mem, norm_g, mem_norm_g, mem_w_kv, ret_w_in, ret_w_out, ret_gn_g, ret_gn_b, ret_decay,
           na_w_in, na_w_out, na_rpb, ffn_w_in, ffn_w_out):
    batch, seq, d = x.shape
    mem_len = mem.shape[1]
    depth = norm_g.shape[0]
    hidden = ffn_w_out.shape[1]
    tm = min(512, seq)
    rows = seq // GRID_W

    mem_k, mem_v = _mem_kv(mem.reshape(batch * mem_len, d), mem_norm_g, mem_w_kv.astype(BF16), mem_len)
    rot = _rotary_tables(seq)
    gains = norm_g.astype(F32)
    ret_w_in, ret_w_out, na_w_in, na_w_out, ffn_w_in, ffn_w_out = (
        w.astype(BF16) for w in (ret_w_in, ret_w_out, na_w_in, na_w_out, ffn_w_in, ffn_w_out))
    h = x.reshape(batch * seq, d)
    for i in range(depth):
        j = i // 2
        if i % 2 == 0:
            proj = _in_proj(h, gains, i, ret_w_in, j, rot, tm, seq)
            log_gamma = -jnp.exp(ret_decay[j].astype(F32))
            y = _retention(proj, log_gamma, ret_gn_g[j], ret_gn_b[j], batch, seq)
            w_out = ret_w_out
        else:
            proj = _in_proj(h, gains, i, na_w_in, j, None, tm, seq)
            y = _neighbourhood(proj, _na_bias_tables(na_rpb[j], rows), batch, seq)
            w_out = na_w_out
        h = _post(y, proj, mem_k, mem_v, h, w_out, j, gains, ffn_w_in, ffn_w_out, i, tm, seq, mem_len)
    return h.reshape(batch, seq, d)
```

```python
import functools

import numpy as np
import jax
import jax.numpy as jnp
from jax import lax
from jax.experimental import pallas as pl
from jax.experimental.pallas import tpu as pltpu

F32 = jnp.float32
BF16 = jnp.bfloat16

LANES = 128
MEM_HEADS = 4
MEM_HEAD_DIM = 64
MEM_WIDTH = MEM_HEADS * MEM_HEAD_DIM
RET_HEADS = 6
RET_HEAD_DIM = 128
RET_CHUNK = 128
RET_ROPE_BASE = 10000.0
RET_UNROLL = 8
GN_EPS = 1e-5
NA_HEADS = 12
NA_HEAD_DIM = 64
GRID_W = 64
NA_WIN_R = 8
NA_WIN_C = 16
NA_ROWS_PER_STEP = 2
NA_STEPS_PER_ITER = 8
FFN_CHUNK = 256
IN_PROJ_ROWS = 1024
POST_ROWS = 1024
EPS = 1e-6
NEG = -1e30
LOG2E = 1.4426950408889634
VMEM_LIMIT = 56 * 1024 * 1024

_NT = (((1,), (1,)), ((), ()))


def _rms(x, g):
    return x * lax.rsqrt(jnp.mean(x * x, axis=-1, keepdims=True) + EPS) * g


def _const_spec(shape):
    nd = len(shape)
    return pl.BlockSpec(shape, lambda *_: (0,) * nd, pipeline_mode=pl.Buffered(1))


def _layer_spec(stacked, layer):
    shape = stacked.shape[1:]
    nd = len(shape)
    return pl.BlockSpec((None,) + shape, lambda *_: (layer,) + (0,) * nd, pipeline_mode=pl.Buffered(1))


def _mem_kv_kernel(mem_ref, g_ref, w_ref, k_ref, v_ref):
    a = _rms(mem_ref[...], g_ref[...]).astype(BF16)
    kv = jnp.dot(a, w_ref[...], preferred_element_type=F32)
    k_ref[...] = kv[:, :MEM_WIDTH].astype(BF16)
    v_ref[...] = kv[:, MEM_WIDTH:].astype(BF16)


def _mem_kv(mem2d, g, w_kv, mem_len):
    m, d = mem2d.shape
    return pl.pallas_call(
        _mem_kv_kernel,
        grid=(m // mem_len,),
        in_specs=[pl.BlockSpec((mem_len, d), lambda i: (i, 0)),
                  _const_spec((1, d)),
                  _const_spec((d, 2 * MEM_WIDTH))],
        out_specs=[pl.BlockSpec((mem_len, MEM_WIDTH), lambda i: (i, 0))] * 2,
        out_shape=[jax.ShapeDtypeStruct((m, MEM_WIDTH), BF16)] * 2,
        name="mem_kv",
    )(mem2d, g.reshape(1, d), w_kv)


def _memory_attention(mq, mk, mv):
    lane = lax.broadcasted_iota(jnp.int32, (1, MEM_WIDTH), 1)
    mq = mq * (MEM_HEAD_DIM ** -0.5)
    m = jnp.zeros(mq.shape, F32)
    for hd in range(MEM_HEADS):
        sel = (lane >= hd * MEM_HEAD_DIM) & (lane < (hd + 1) * MEM_HEAD_DIM)
        qh = jnp.where(sel, mq, jnp.zeros_like(mq))
        sc = lax.dot_general(qh, mk, _NT, preferred_element_type=F32)
        sc = sc - jnp.max(sc, axis=-1, keepdims=True)
        p = jnp.exp(sc)
        l = jnp.sum(p, axis=-1, keepdims=True)
        vh = jnp.where(sel, mv, jnp.zeros_like(mv))
        m += jnp.dot(p.astype(BF16), vh, preferred_element_type=F32) / l
    return m


def _in_proj_kernel(h_ref, g_ref, w_ref, *rest, n_rot):
    if n_rot:
        cos_ref, sin_ref, o_ref = rest
    else:
        (o_ref,) = rest
    a = _rms(h_ref[...], g_ref[0:1, :]).astype(BF16)
    n = o_ref.shape[1]
    step = 2 * LANES
    for c in range(n // step):
        y = jnp.dot(a, w_ref[:, c * step:(c + 1) * step], preferred_element_type=F32)
        if 2 * c < n_rot:
            halves = []
            for y1 in (y[:, :LANES], y[:, LANES:]):
                halves.append(y1 * cos_ref[...] + pltpu.roll(y1, LANES // 2, 1) * sin_ref[...])
            y = jnp.concatenate(halves, axis=1)
        o_ref[:, c * step:(c + 1) * step] = y.astype(BF16)


def _in_proj(h, gains, layer, w, w_layer, rot, tm, seq):
    m, d = h.shape
    n = w.shape[2]
    in_specs = [pl.BlockSpec((tm, d), lambda i: (i, 0)), _layer_spec(gains, layer), _layer_spec(w, w_layer)]
    args = [h, gains, w]
    n_rot = 0
    if rot is not None:
        n_rot = 2 * RET_HEADS
        tiles_per_seq = seq // tm
        in_specs += [pl.BlockSpec((tm, LANES), lambda i: (i % tiles_per_seq, 0))] * 2
        args += list(rot)
    return pl.pallas_call(
        functools.partial(_in_proj_kernel, n_rot=n_rot),
        grid=(m // tm,),
        in_specs=in_specs,
        out_specs=pl.BlockSpec((tm, n), lambda i: (i, 0)),
        out_shape=jax.ShapeDtypeStruct((m, n), BF16),
        compiler_params=pltpu.CompilerParams(dimension_semantics=("arbitrary",),
                                             vmem_limit_bytes=VMEM_LIMIT),
        name="in_proj",
    )(*args)


def _rotary_tables(seq):
    half = RET_HEAD_DIM // 2
    inv = RET_ROPE_BASE ** (-jnp.arange(half, dtype=F32) / half)
    ang = jnp.arange(seq, dtype=F32)[:, None] * inv[None, :]
    c, s = jnp.cos(ang), jnp.sin(ang)
    return jnp.concatenate([c, c], axis=1), jnp.concatenate([-s, s], axis=1)


def _ret_kernel(lg_ref, q_ref, k_ref, v_ref, g_ref, gng_ref, gnb_ref, o_ref, p_ref, y_ref, kv_ref, st_ref):
    hd = pl.program_id(1)
    c = RET_CHUNK
    dk = RET_HEAD_DIM
    n_chunks = q_ref.shape[0] // c
    lg_f = lg_ref[0, hd]
    lg_b = lg_ref[1, hd]
    scale = RET_HEAD_DIM ** -0.5

    row = lax.broadcasted_iota(jnp.int32, (c, c), 0)
    col = lax.broadcasted_iota(jnp.int32, (c, c), 1)
    rel = (row - col).astype(F32)
    decay = (jnp.where(rel >= 0, jnp.exp(lg_f * jnp.maximum(rel, 0.0)), 0.0)
             + jnp.where(rel <= 0, jnp.exp(lg_b * jnp.maximum(-rel, 0.0)), 0.0)) * scale
    tok_row = row.astype(F32)
    tok_col = col.astype(F32)
    cd_f = jnp.exp(jnp.full((1, 1), c, F32) * lg_f)
    cd_b = jnp.exp(jnp.full((1, 1), c, F32) * lg_b)

    def chunk(n):
        return pl.ds(pl.multiple_of(n * c, c), c)

    kdec_f = jnp.exp(lg_f * (c - 1 - tok_col))
    kdec_b = jnp.exp(lg_b * tok_col)

    group = RET_UNROLL
    n_groups = n_chunks // group

    def kv_chunk(n):
        kt = k_ref[chunk(n), :].astype(F32).T
        kd = jnp.concatenate([(kt * kdec_f).astype(BF16), (kt * kdec_b).astype(BF16)], axis=0)
        kv_ref[n] = jnp.dot(kd, v_ref[chunk(n), :], preferred_element_type=F32)
        s = jnp.dot(q_ref[chunk(n), :], kt.astype(BF16), preferred_element_type=F32) * decay
        p_ref[chunk(n), :] = s.astype(BF16)

    def fwd_step(n, s_f):
        st_ref[n, :dk, :] = s_f.astype(BF16)
        return s_f * cd_f + kv_ref[n, :dk, :]

    def bwd_step(n, s_b):
        st_ref[n, dk:, :] = s_b.astype(BF16)
        return s_b * cd_b + kv_ref[n, dk:, :]

    for j in range(group):
        kv_chunk(j)

    def kv_body(g, s_f):
        for j in range(group):
            s_f = fwd_step(g * group + j, s_f)
            kv_chunk((g + 1) * group + j)
        return s_f

    zero = jnp.zeros((dk, RET_HEAD_DIM), F32)
    s_f = lax.fori_loop(0, n_groups - 1, kv_body, zero)
    for j in range(group):
        s_f = fwd_step((n_groups - 1) * group + j, s_f)
    lax.fori_loop(0, n_chunks, lambda i, s_b: bwd_step(n_chunks - 1 - i, s_b), zero, unroll=RET_UNROLL)

    qdec_f = jnp.exp(lg_f * (tok_row + 1)) * scale
    qdec_b = jnp.exp(lg_b * (c - tok_row)) * scale

    def y_chunk(n):
        qf32 = q_ref[chunk(n), :].astype(F32)
        qd = jnp.concatenate([(qf32 * qdec_f).astype(BF16), (qf32 * qdec_b).astype(BF16)], axis=1)
        y_ref[chunk(n), :] = (jnp.dot(p_ref[chunk(n), :], v_ref[chunk(n), :], preferred_element_type=F32)
                              + jnp.dot(qd, st_ref[n], preferred_element_type=F32))

    def norm_chunk(n):
        y = y_ref[chunk(n), :]
        mu = jnp.mean(y, axis=-1, keepdims=True)
        var = jnp.maximum(jnp.mean(y * y, axis=-1, keepdims=True) - mu * mu, 0.0)
        yn = (y - mu) * lax.rsqrt(var + GN_EPS) * gng_ref[...] + gnb_ref[...]
        gate = g_ref[chunk(n), :].astype(F32)
        o_ref[chunk(n), :] = (gate * jax.nn.sigmoid(gate) * yn).astype(BF16)

    for j in range(group):
        y_chunk(j)

    def pipe_body(g, carry):
        for j in range(group):
            norm_chunk(g * group + j)
            y_chunk((g + 1) * group + j)
        return carry

    lax.fori_loop(0, n_groups - 1, pipe_body, 0)
    for j in range(group):
        norm_chunk((n_groups - 1) * group + j)


def _retention(proj, log_gamma, gn_g, gn_b, batch, seq):
    m, _ = proj.shape
    n_chunks = seq // RET_CHUNK
    hblk = lambda off: pl.BlockSpec((seq, RET_HEAD_DIM), lambda b, h: (b, off + h))
    vec = pl.BlockSpec((1, RET_HEAD_DIM), lambda b, h: (0, h))
    tok = RET_HEADS * RET_HEAD_DIM
    return pl.pallas_call(
        _ret_kernel,
        grid=(batch, RET_HEADS),
        in_specs=[pl.BlockSpec(memory_space=pltpu.SMEM),
                  hblk(0), hblk(RET_HEADS), hblk(2 * RET_HEADS), hblk(3 * RET_HEADS), vec, vec],
        out_specs=pl.BlockSpec((seq, RET_HEAD_DIM), lambda b, h: (b, h)),
        out_shape=jax.ShapeDtypeStruct((m, tok), BF16),
        scratch_shapes=[pltpu.VMEM((seq, RET_CHUNK), BF16),
                        pltpu.VMEM((seq, RET_HEAD_DIM), F32),
                        pltpu.VMEM((n_chunks, 2 * RET_HEAD_DIM, RET_HEAD_DIM), F32),
                        pltpu.VMEM((n_chunks, 2 * RET_HEAD_DIM, RET_HEAD_DIM), BF16)],
        compiler_params=pltpu.CompilerParams(dimension_semantics=("arbitrary", "arbitrary"),
                                             vmem_limit_bytes=VMEM_LIMIT),
        name="retention",
    )(log_gamma, proj, proj, proj, proj, gn_g.reshape(1, tok), gn_b.reshape(1, tok))


def _na_step_geometry(rows):
    rg = NA_ROWS_PER_STEP
    wr = min(NA_WIN_R, rows)
    win = wr + rg - 1
    steps = rows // rg
    starts = [int(np.clip(np.clip(s * rg - wr // 2, 0, rows - wr), 0, rows - win)) for s in range(steps)]
    reps, types = [], []
    for s in range(steps):
        sig = (starts[s] - s * rg,
               tuple(int(np.clip(s * rg + j - wr // 2, 0, rows - wr)) - s * rg for j in range(rg)))
        for t, (sig_t, _) in enumerate(reps):
            if sig_t == sig:
                types.append(t)
                break
        else:
            types.append(len(reps))
            reps.append((sig, s))
    return starts, types, [s for _, s in reps], win


def _na_bias_tables(rpb, rows):
    rg = NA_ROWS_PER_STEP
    wr = min(NA_WIN_R, rows)
    starts, _, rep_steps, win = _na_step_geometry(rows)
    heads, n_dr, n_dc = rpb.shape
    assert 2 * GRID_W == LANES and n_dc <= GRID_W
    pad = rg + wr
    lo = np.zeros((len(rep_steps), rg), np.int32)
    row_ok = np.zeros((len(rep_steps), rg * win), np.int32)
    for t, s in enumerate(rep_steps):
        for j in range(rg):
            qrow = s * rg + j
            lo[t, j] = starts[s] - qrow + NA_WIN_R - 1 + pad
            rs = int(np.clip(qrow - wr // 2, 0, rows - wr))
            krow = starts[s] + np.arange(win)
            row_ok[t, j * win:(j + 1) * win] = (krow >= rs) & (krow < rs + wr)
    rpb_rows = jnp.pad(rpb.astype(F32), ((0, 0), (pad, pad), (0, LANES - n_dc)))
    smem = pl.BlockSpec(memory_space=pltpu.SMEM)
    return pl.pallas_call(
        functools.partial(_na_table_kernel, win=win),
        grid=(heads // 2, len(rep_steps)),
        in_specs=[smem, smem, pl.BlockSpec((2, n_dr + 2 * pad, LANES), lambda p, t: (p, 0, 0))],
        out_specs=pl.BlockSpec((None, None, 2 * rg * GRID_W, win * GRID_W), lambda p, t: (p, t, 0, 0)),
        out_shape=jax.ShapeDtypeStruct((heads // 2, len(rep_steps), 2 * rg * GRID_W, win * GRID_W), F32),
        name="na_tables",
    )(jnp.asarray(lo), jnp.asarray(row_ok), rpb_rows)


def _na_table_kernel(lo_ref, ok_ref, r_ref, o_ref, *, win):
    t = pl.program_id(1)
    rg = NA_ROWS_PER_STEP
    shape = (GRID_W, LANES)
    qc = lax.broadcasted_iota(jnp.int32, shape, 0)
    lane = lax.broadcasted_iota(jnp.int32, shape, 1)
    kc = lane & (GRID_W - 1)
    cs = jnp.clip(qc - NA_WIN_C // 2, 0, GRID_W - NA_WIN_C)
    col_ok = (kc >= cs) & (kc < cs + NA_WIN_C)
    for hd in range(2):
        for j in range(rg):
            base = lo_ref[t, j]
            rows = slice((hd * rg + j) * GRID_W, (hd * rg + j + 1) * GRID_W)
            for w0 in range(0, win, 2):
                pair = w0 + 1 < win
                x = r_ref[hd, pl.ds(base + w0, 1), :]
                ok_lo = ok_ref[t, j * win + w0]
                ok_hi = ok_ref[t, j * win + w0 + 1] if pair else ok_lo
                if pair:
                    x = x + pltpu.roll(r_ref[hd, pl.ds(base + w0 + 1, 1), :], GRID_W, 1)
                row_ok = jnp.where(lane < GRID_W, ok_lo, ok_hi) != 0
                band = pltpu.roll(jnp.broadcast_to(x, shape), LANES - (NA_WIN_C - 1), 1,
                                  stride=1, stride_axis=0)
                val = jnp.where(col_ok & row_ok, band * LOG2E, NEG)
                if pair:
                    o_ref[rows, w0 * GRID_W:(w0 + 2) * GRID_W] = val
                else:
                    o_ref[rows, w0 * GRID_W:(w0 + 1) * GRID_W] = val[:, :GRID_W]


def _na_kernel(start_ref, type_ref, q_ref, k_ref, v_ref, tab_ref, o_ref, p_ref, r_ref, *, win):
    nq = NA_ROWS_PER_STEP * GRID_W
    nk = win * GRID_W
    steps = q_ref.shape[0] // nq
    lo = lax.broadcasted_iota(jnp.int32, (1, LANES), 1) < NA_HEAD_DIM
    scale = NA_HEAD_DIM ** -0.5 * LOG2E

    def rows(s):
        return pl.ds(pl.multiple_of(s * nq, nq), nq)

    def window(s):
        return pl.ds(pl.multiple_of(start_ref[s] * GRID_W, GRID_W), nk)

    def softmax(s, slot):
        q = (q_ref[rows(s), :].astype(F32) * scale).astype(BF16)
        zero = jnp.zeros_like(q)
        qh = jnp.concatenate([jnp.where(lo, q, zero), jnp.where(lo, zero, q)], axis=0)
        sc = lax.dot_general(qh, k_ref[window(s), :], _NT, preferred_element_type=F32)
        sc = sc + tab_ref[type_ref[s]]
        p = jnp.exp2(sc - jnp.max(sc, axis=-1, keepdims=True))
        p_ref[slot] = p.astype(BF16)
        r_ref[slot] = jnp.broadcast_to(1.0 / jnp.sum(p, axis=-1, keepdims=True), (2 * nq, LANES))

    def attend(s, slot):
        o = jnp.dot(p_ref[slot], v_ref[window(s), :], preferred_element_type=F32) * r_ref[slot]
        o_ref[rows(s), :] = jnp.where(lo, o[:nq], o[nq:]).astype(BF16)

    per_iter = NA_STEPS_PER_ITER
    softmax(0, 0)

    def run(s0, count, last):
        for j in range(count):
            if not (last and j == count - 1):
                softmax(s0 + j + 1, (j + 1) % 2)
            attend(s0 + j, j % 2)

    def body(i, carry):
        run(i * per_iter, per_iter, False)
        return carry

    lax.fori_loop(0, steps // per_iter - 1, body, 0)
    run(steps - per_iter, per_iter, True)


def _neighbourhood(proj, tables, batch, seq):
    m, _ = proj.shape
    rows = seq // GRID_W
    starts, types, _, win = _na_step_geometry(rows)
    pairs = NA_HEADS // 2
    tok = NA_HEADS * NA_HEAD_DIM
    n_types, nq2, nk = tables.shape[1:]
    hblk = lambda off: pl.BlockSpec((seq, LANES), lambda b, p: (b, off + p))
    smem = pl.BlockSpec(memory_space=pltpu.SMEM)
    return pl.pallas_call(
        functools.partial(_na_kernel, win=win),
        grid=(batch, pairs),
        in_specs=[smem, smem, hblk(0), hblk(pairs), hblk(2 * pairs),
                  pl.BlockSpec((None, n_types, nq2, nk), lambda b, p: (p, 0, 0, 0))],
        out_specs=pl.BlockSpec((seq, LANES), lambda b, p: (b, p)),
        out_shape=jax.ShapeDtypeStruct((m, tok), BF16),
        scratch_shapes=[pltpu.VMEM((2, nq2, nk), BF16), pltpu.VMEM((2, nq2, LANES), F32)],
        compiler_params=pltpu.CompilerParams(dimension_semantics=("arbitrary", "arbitrary"),
                                             vmem_limit_bytes=VMEM_LIMIT),
        name="neighbourhood",
    )(jnp.asarray(starts, jnp.int32), jnp.asarray(types, jnp.int32), proj, proj, proj, tables)


def _post_kernel(y_ref, mq_ref, mk_ref, mv_ref, h_ref, wo_ref, g_ref, w1_ref, w2_ref, o_ref, f_ref):
    tok = y_ref.shape[1]
    hidden = w2_ref.shape[0]
    m = _memory_attention(mq_ref[...], mk_ref[...], mv_ref[...])
    y = jnp.dot(y_ref[...], wo_ref[:tok, :], preferred_element_type=F32)
    y += jnp.dot(m.astype(BF16), wo_ref[tok:, :], preferred_element_type=F32)
    h1 = h_ref[...] + _rms(y, g_ref[1:2, :])
    a = _rms(h1, g_ref[2:3, :]).astype(BF16)

    for c in range(hidden // FFN_CHUNK):
        cols = slice(c * FFN_CHUNK, (c + 1) * FFN_CHUNK)
        gate = jnp.dot(a, w1_ref[:, cols], preferred_element_type=F32)
        up = jnp.dot(a, w1_ref[:, hidden + c * FFN_CHUNK:hidden + (c + 1) * FFN_CHUNK],
                     preferred_element_type=F32)
        act = (gate * jax.nn.sigmoid(gate) * up).astype(BF16)
        part = jnp.dot(act, w2_ref[cols, :], preferred_element_type=F32)
        if c == 0:
            f_ref[...] = part
        else:
            f_ref[...] += part
    o_ref[...] = h1 + _rms(f_ref[...], g_ref[3:4, :])


def _post(y, proj, mem_k, mem_v, h, w_out, w_out_layer, gains, w1, w2, layer, tm, seq, mem_len):
    m, d = h.shape
    tok = y.shape[1]
    mq_block = (proj.shape[1] - MEM_WIDTH) // MEM_WIDTH
    tiles_per_seq = seq // tm
    hidden = w2.shape[1]
    assert hidden % FFN_CHUNK == 0 and w1.shape[1:] == (d, 2 * hidden)
    return pl.pallas_call(
        _post_kernel,
        grid=(m // tm,),
        in_specs=[pl.BlockSpec((tm, tok), lambda i: (i, 0)),
                  pl.BlockSpec((tm, MEM_WIDTH), lambda i: (i, mq_block)),
                  pl.BlockSpec((mem_len, MEM_WIDTH), lambda i: (i // tiles_per_seq, 0)),
                  pl.BlockSpec((mem_len, MEM_WIDTH), lambda i: (i // tiles_per_seq, 0)),
                  pl.BlockSpec((tm, d), lambda i: (i, 0)),
                  _layer_spec(w_out, w_out_layer),
                  _layer_spec(gains, layer),
                  _layer_spec(w1, layer),
                  _layer_spec(w2, layer)],
        out_specs=pl.BlockSpec((tm, d), lambda i: (i, 0)),
        out_shape=jax.ShapeDtypeStruct((m, d), F32),
        scratch_shapes=[pltpu.VMEM((tm, d), F32)],
        compiler_params=pltpu.CompilerParams(dimension_semantics=("arbitrary",),
                                             vmem_limit_bytes=VMEM_LIMIT),
        name="post",
    )(y, proj, mem_k, mem_v, h, w_out, gains, w1, w2)


def kernel(x, mem, norm_g, mem_norm_g, mem_w_kv, ret_w_in, ret_w_out, ret_gn_g, ret_gn_b, ret_decay,
           na_w_in, na_w_out, na_rpb, ffn_w_in, ffn_w_out):
    batch, seq, d = x.shape
    mem_len = mem.shape[1]
    depth = norm_g.shape[0]
    hidden = ffn_w_out.shape[1]
    tm_in = min(IN_PROJ_ROWS, seq)
    tm_post = min(POST_ROWS, seq)
    rows = seq // GRID_W

    mem_k, mem_v = _mem_kv(mem.reshape(batch * mem_len, d), mem_norm_g, mem_w_kv.astype(BF16), mem_len)
    rot = _rotary_tables(seq)
    gains = norm_g.astype(F32)
    ret_w_in, ret_w_out, na_w_in, na_w_out, ffn_w_in, ffn_w_out = (
        w.astype(BF16) for w in (ret_w_in, ret_w_out, na_w_in, na_w_out, ffn_w_in, ffn_w_out))
    h = x.reshape(batch * seq, d)
    for i in range(depth):
        j = i // 2
        if i % 2 == 0:
            proj = _in_proj(h, gains, i, ret_w_in, j, rot, tm_in, seq)
            log_gamma = -jnp.exp(ret_decay[j].astype(F32))
            y = _retention(proj, log_gamma, ret_gn_g[j], ret_gn_b[j], batch, seq)
            w_out = ret_w_out
        else:
            proj = _in_proj(h, gains, i, na_w_in, j, None, tm_in, seq)
            y = _neighbourhood(proj, _na_bias_tables(na_rpb[j], rows), batch, seq)
            w_out = na_w_out
        h = _post(y, proj, mem_k, mem_v, h, w_out, j, gains, ffn_w_in, ffn_w_out, i, tm_post, seq,
                  mem_len)
    return h.reshape(batch, seq, d)
```

```python
import functools

import numpy as np
import jax
import jax.numpy as jnp
from jax import lax
from jax.experimental import pallas as pl
from jax.experimental.pallas import tpu as pltpu

F32 = jnp.float32
BF16 = jnp.bfloat16

LANES = 128
MEM_HEADS = 4
MEM_HEAD_DIM = 64
MEM_WIDTH = MEM_HEADS * MEM_HEAD_DIM
RET_HEADS = 6
RET_HEAD_DIM = 128
RET_CHUNK = 128
RET_ROPE_BASE = 10000.0
RET_UNROLL = 16
GN_EPS = 1e-5
NA_HEADS = 12
NA_HEAD_DIM = 64
GRID_W = 64
NA_WIN_R = 8
NA_WIN_C = 16
NA_ROWS_PER_STEP = 2
NA_STEPS_PER_ITER = 16
FFN_CHUNK = 256
IN_PROJ_ROWS = 1024
POST_ROWS = 1024
EPS = 1e-6
NEG = -1e30
LOG2E = 1.4426950408889634
VMEM_LIMIT = 56 * 1024 * 1024

_NT = (((1,), (1,)), ((), ()))


def _rms(x, g):
    return x * lax.rsqrt(jnp.mean(x * x, axis=-1, keepdims=True) + EPS) * g


def _const_spec(shape):
    nd = len(shape)
    return pl.BlockSpec(shape, lambda *_: (0,) * nd, pipeline_mode=pl.Buffered(1))


def _layer_spec(stacked, layer):
    shape = stacked.shape[1:]
    nd = len(shape)
    return pl.BlockSpec((None,) + shape, lambda *_: (layer,) + (0,) * nd, pipeline_mode=pl.Buffered(1))


def _mem_kv_kernel(mem_ref, g_ref, w_ref, k_ref, v_ref):
    a = _rms(mem_ref[...], g_ref[...]).astype(BF16)
    kv = jnp.dot(a, w_ref[...], preferred_element_type=F32)
    k_ref[...] = kv[:, :MEM_WIDTH].astype(BF16)
    v_ref[...] = kv[:, MEM_WIDTH:].astype(BF16)


def _mem_kv(mem2d, g, w_kv, mem_len):
    m, d = mem2d.shape
    return pl.pallas_call(
        _mem_kv_kernel,
        grid=(m // mem_len,),
        in_specs=[pl.BlockSpec((mem_len, d), lambda i: (i, 0)),
                  _const_spec((1, d)),
                  _const_spec((d, 2 * MEM_WIDTH))],
        out_specs=[pl.BlockSpec((mem_len, MEM_WIDTH), lambda i: (i, 0))] * 2,
        out_shape=[jax.ShapeDtypeStruct((m, MEM_WIDTH), BF16)] * 2,
        name="mem_kv",
    )(mem2d, g.reshape(1, d), w_kv)


def _memory_attention(mq, mk, mv):
    lane = lax.broadcasted_iota(jnp.int32, (1, MEM_WIDTH), 1)
    mq = mq * (MEM_HEAD_DIM ** -0.5)
    m = jnp.zeros(mq.shape, F32)
    for hd in range(MEM_HEADS):
        sel = (lane >= hd * MEM_HEAD_DIM) & (lane < (hd + 1) * MEM_HEAD_DIM)
        qh = jnp.where(sel, mq, jnp.zeros_like(mq))
        sc = lax.dot_general(qh, mk, _NT, preferred_element_type=F32)
        sc = sc - jnp.max(sc, axis=-1, keepdims=True)
        p = jnp.exp(sc)
        l = jnp.sum(p, axis=-1, keepdims=True)
        vh = jnp.where(sel, mv, jnp.zeros_like(mv))
        m += jnp.dot(p.astype(BF16), vh, preferred_element_type=F32) / l
    return m


def _in_proj_kernel(h_ref, g_ref, w_ref, *rest, n_rot):
    if n_rot:
        cos_ref, sin_ref, o_ref = rest
    else:
        (o_ref,) = rest
    a = _rms(h_ref[...], g_ref[0:1, :]).astype(BF16)
    n = o_ref.shape[1]
    step = 2 * LANES
    for c in range(n // step):
        y = jnp.dot(a, w_ref[:, c * step:(c + 1) * step], preferred_element_type=F32)
        if 2 * c < n_rot:
            halves = []
            for y1 in (y[:, :LANES], y[:, LANES:]):
                halves.append(y1 * cos_ref[...] + pltpu.roll(y1, LANES // 2, 1) * sin_ref[...])
            y = jnp.concatenate(halves, axis=1)
        o_ref[:, c * step:(c + 1) * step] = y.astype(BF16)


def _in_proj(h, gains, layer, w, w_layer, rot, tm, seq):
    m, d = h.shape
    n = w.shape[2]
    in_specs = [pl.BlockSpec((tm, d), lambda i: (i, 0)), _layer_spec(gains, layer), _layer_spec(w, w_layer)]
    args = [h, gains, w]
    n_rot = 0
    if rot is not None:
        n_rot = 2 * RET_HEADS
        tiles_per_seq = seq // tm
        in_specs += [pl.BlockSpec((tm, LANES), lambda i: (i % tiles_per_seq, 0))] * 2
        args += list(rot)
    return pl.pallas_call(
        functools.partial(_in_proj_kernel, n_rot=n_rot),
        grid=(m // tm,),
        in_specs=in_specs,
        out_specs=pl.BlockSpec((tm, n), lambda i: (i, 0)),
        out_shape=jax.ShapeDtypeStruct((m, n), BF16),
        compiler_params=pltpu.CompilerParams(dimension_semantics=("arbitrary",),
                                             vmem_limit_bytes=VMEM_LIMIT),
        name="in_proj",
    )(*args)


def _rotary_tables(seq):
    half = RET_HEAD_DIM // 2
    inv = RET_ROPE_BASE ** (-jnp.arange(half, dtype=F32) / half)
    ang = jnp.arange(seq, dtype=F32)[:, None] * inv[None, :]
    c, s = jnp.cos(ang), jnp.sin(ang)
    return jnp.concatenate([c, c], axis=1), jnp.concatenate([-s, s], axis=1)


def _ret_kernel(lg_ref, q_ref, k_ref, v_ref, g_ref, gng_ref, gnb_ref, o_ref, p_ref, y_ref, kv_ref, st_ref):
    hd = pl.program_id(1)
    c = RET_CHUNK
    dk = RET_HEAD_DIM
    n_chunks = q_ref.shape[0] // c
    lg_f = lg_ref[0, hd]
    lg_b = lg_ref[1, hd]
    scale = RET_HEAD_DIM ** -0.5

    row = lax.broadcasted_iota(jnp.int32, (c, c), 0)
    col = lax.broadcasted_iota(jnp.int32, (c, c), 1)
    rel = (row - col).astype(F32)
    decay = (jnp.where(rel >= 0, jnp.exp(lg_f * jnp.maximum(rel, 0.0)), 0.0)
             + jnp.where(rel <= 0, jnp.exp(lg_b * jnp.maximum(-rel, 0.0)), 0.0)) * scale
    tok_row = row.astype(F32)
    tok_col = col.astype(F32)
    cd_f = jnp.exp(jnp.full((1, 1), c, F32) * lg_f)
    cd_b = jnp.exp(jnp.full((1, 1), c, F32) * lg_b)

    def chunk(n):
        return pl.ds(pl.multiple_of(n * c, c), c)

    kdec_f = jnp.exp(lg_f * (c - 1 - tok_col))
    kdec_b = jnp.exp(lg_b * tok_col)

    group = min(RET_UNROLL, n_chunks)
    n_groups = n_chunks // group

    def kv_chunk(n):
        kt = k_ref[chunk(n), :].astype(F32).T
        kd = jnp.concatenate([(kt * kdec_f).astype(BF16), (kt * kdec_b).astype(BF16)], axis=0)
        kv_ref[n] = jnp.dot(kd, v_ref[chunk(n), :], preferred_element_type=F32)
        s = jnp.dot(q_ref[chunk(n), :], kt.astype(BF16), preferred_element_type=F32) * decay
        p_ref[chunk(n), :] = s.astype(BF16)

    def fwd_step(n, s_f):
        st_ref[n, :dk, :] = s_f.astype(BF16)
        return s_f * cd_f + kv_ref[n, :dk, :]

    def bwd_step(n, s_b):
        st_ref[n, dk:, :] = s_b.astype(BF16)
        return s_b * cd_b + kv_ref[n, dk:, :]

    for j in range(group):
        kv_chunk(j)

    def kv_body(g, s_f):
        for j in range(group):
            s_f = fwd_step(g * group + j, s_f)
            kv_chunk((g + 1) * group + j)
        return s_f

    zero = jnp.zeros((dk, RET_HEAD_DIM), F32)
    s_f = lax.fori_loop(0, n_groups - 1, kv_body, zero)
    for j in range(group):
        s_f = fwd_step((n_groups - 1) * group + j, s_f)
    lax.fori_loop(0, n_chunks, lambda i, s_b: bwd_step(n_chunks - 1 - i, s_b), zero, unroll=RET_UNROLL)

    qdec_f = jnp.exp(lg_f * (tok_row + 1)) * scale
    qdec_b = jnp.exp(lg_b * (c - tok_row)) * scale

    def y_chunk(n):
        qf32 = q_ref[chunk(n), :].astype(F32)
        qd = jnp.concatenate([(qf32 * qdec_f).astype(BF16), (qf32 * qdec_b).astype(BF16)], axis=1)
        y_ref[chunk(n), :] = (jnp.dot(p_ref[chunk(n), :], v_ref[chunk(n), :], preferred_element_type=F32)
                              + jnp.dot(qd, st_ref[n], preferred_element_type=F32))

    def norm_chunk(n):
        y = y_ref[chunk(n), :]
        mu = jnp.mean(y, axis=-1, keepdims=True)
        var = jnp.maximum(jnp.mean(y * y, axis=-1, keepdims=True) - mu * mu, 0.0)
        yn = (y - mu) * lax.rsqrt(var + GN_EPS) * gng_ref[...] + gnb_ref[...]
        gate = g_ref[chunk(n), :].astype(F32)
        o_ref[chunk(n), :] = (gate * jax.nn.sigmoid(gate) * yn).astype(BF16)

    for j in range(group):
        y_chunk(j)

    def pipe_body(g, carry):
        for j in range(group):
            norm_chunk(g * group + j)
            y_chunk((g + 1) * group + j)
        return carry

    lax.fori_loop(0, n_groups - 1, pipe_body, 0)
    for j in range(group):
        norm_chunk((n_groups - 1) * group + j)


def _retention(proj, log_gamma, gn_g, gn_b, batch, seq):
    m, _ = proj.shape
    n_chunks = seq // RET_CHUNK
    hblk = lambda off: pl.BlockSpec((seq, RET_HEAD_DIM), lambda b, h: (b, off + h))
    vec = pl.BlockSpec((1, RET_HEAD_DIM), lambda b, h: (0, h))
    tok = RET_HEADS * RET_HEAD_DIM
    return pl.pallas_call(
        _ret_kernel,
        grid=(batch, RET_HEADS),
        in_specs=[pl.BlockSpec(memory_space=pltpu.SMEM),
                  hblk(0), hblk(RET_HEADS), hblk(2 * RET_HEADS), hblk(3 * RET_HEADS), vec, vec],
        out_specs=pl.BlockSpec((seq, RET_HEAD_DIM), lambda b, h: (b, h)),
        out_shape=jax.ShapeDtypeStruct((m, tok), BF16),
        scratch_shapes=[pltpu.VMEM((seq, RET_CHUNK), BF16),
                        pltpu.VMEM((seq, RET_HEAD_DIM), F32),
                        pltpu.VMEM((n_chunks, 2 * RET_HEAD_DIM, RET_HEAD_DIM), F32),
                        pltpu.VMEM((n_chunks, 2 * RET_HEAD_DIM, RET_HEAD_DIM), BF16)],
        compiler_params=pltpu.CompilerParams(dimension_semantics=("arbitrary", "arbitrary"),
                                             vmem_limit_bytes=VMEM_LIMIT),
        name="retention",
    )(log_gamma, proj, proj, proj, proj, gn_g.reshape(1, tok), gn_b.reshape(1, tok))


def _na_step_geometry(rows):
    rg = NA_ROWS_PER_STEP
    wr = min(NA_WIN_R, rows)
    win = wr + rg - 1
    steps = rows // rg
    starts = [int(np.clip(np.clip(s * rg - wr // 2, 0, rows - wr), 0, rows - win)) for s in range(steps)]
    reps, types = [], []
    for s in range(steps):
        sig = (starts[s] - s * rg,
               tuple(int(np.clip(s * rg + j - wr // 2, 0, rows - wr)) - s * rg for j in range(rg)))
        for t, (sig_t, _) in enumerate(reps):
            if sig_t == sig:
                types.append(t)
                break
        else:
            types.append(len(reps))
            reps.append((sig, s))
    return starts, types, [s for _, s in reps], win


def _na_bias_tables(rpb, rows):
    rg = NA_ROWS_PER_STEP
    wr = min(NA_WIN_R, rows)
    starts, _, rep_steps, win = _na_step_geometry(rows)
    heads, n_dr, n_dc = rpb.shape
    assert 2 * GRID_W == LANES and n_dc <= GRID_W
    pad = rg + wr
    lo = np.zeros((len(rep_steps), rg), np.int32)
    row_ok = np.zeros((len(rep_steps), rg * win), np.int32)
    for t, s in enumerate(rep_steps):
        for j in range(rg):
            qrow = s * rg + j
            lo[t, j] = starts[s] - qrow + NA_WIN_R - 1 + pad
            rs = int(np.clip(qrow - wr // 2, 0, rows - wr))
            krow = starts[s] + np.arange(win)
            row_ok[t, j * win:(j + 1) * win] = (krow >= rs) & (krow < rs + wr)
    rpb_rows = jnp.pad(rpb.astype(F32), ((0, 0), (pad, pad), (0, LANES - n_dc)))
    smem = pl.BlockSpec(memory_space=pltpu.SMEM)
    return pl.pallas_call(
        functools.partial(_na_table_kernel, win=win),
        grid=(heads // 2, len(rep_steps)),
        in_specs=[smem, smem, pl.BlockSpec((2, n_dr + 2 * pad, LANES), lambda p, t: (p, 0, 0))],
        out_specs=pl.BlockSpec((None, None, 2 * rg * GRID_W, win * GRID_W), lambda p, t: (p, t, 0, 0)),
        out_shape=jax.ShapeDtypeStruct((heads // 2, len(rep_steps), 2 * rg * GRID_W, win * GRID_W), F32),
        name="na_tables",
    )(jnp.asarray(lo), jnp.asarray(row_ok), rpb_rows)


def _na_table_kernel(lo_ref, ok_ref, r_ref, o_ref, *, win):
    t = pl.program_id(1)
    rg = NA_ROWS_PER_STEP
    shape = (GRID_W, LANES)
    qc = lax.broadcasted_iota(jnp.int32, shape, 0)
    lane = lax.broadcasted_iota(jnp.int32, shape, 1)
    kc = lane & (GRID_W - 1)
    cs = jnp.clip(qc - NA_WIN_C // 2, 0, GRID_W - NA_WIN_C)
    col_ok = (kc >= cs) & (kc < cs + NA_WIN_C)
    for hd in range(2):
        for j in range(rg):
            base = lo_ref[t, j]
            rows = slice((hd * rg + j) * GRID_W, (hd * rg + j + 1) * GRID_W)
            for w0 in range(0, win, 2):
                pair = w0 + 1 < win
                x = r_ref[hd, pl.ds(base + w0, 1), :]
                ok_lo = ok_ref[t, j * win + w0]
                ok_hi = ok_ref[t, j * win + w0 + 1] if pair else ok_lo
                if pair:
                    x = x + pltpu.roll(r_ref[hd, pl.ds(base + w0 + 1, 1), :], GRID_W, 1)
                row_ok = jnp.where(lane < GRID_W, ok_lo, ok_hi) != 0
                band = pltpu.roll(jnp.broadcast_to(x, shape), LANES - (NA_WIN_C - 1), 1,
                                  stride=1, stride_axis=0)
                val = jnp.where(col_ok & row_ok, band * LOG2E, NEG)
                if pair:
                    o_ref[rows, w0 * GRID_W:(w0 + 2) * GRID_W] = val
                else:
                    o_ref[rows, w0 * GRID_W:(w0 + 1) * GRID_W] = val[:, :GRID_W]


def _na_kernel(start_ref, type_ref, q_ref, k_ref, v_ref, tab_ref, o_ref, p_ref, r_ref, *, win):
    nq = NA_ROWS_PER_STEP * GRID_W
    nk = win * GRID_W
    steps = q_ref.shape[0] // nq
    lo = lax.broadcasted_iota(jnp.int32, (1, LANES), 1) < NA_HEAD_DIM
    scale = NA_HEAD_DIM ** -0.5 * LOG2E

    def rows(s):
        return pl.ds(pl.multiple_of(s * nq, nq), nq)

    def window(s):
        return pl.ds(pl.multiple_of(start_ref[s] * GRID_W, GRID_W), nk)

    def softmax(s, slot):
        q = (q_ref[rows(s), :].astype(F32) * scale).astype(BF16)
        zero = jnp.zeros_like(q)
        qh = jnp.concatenate([jnp.where(lo, q, zero), jnp.where(lo, zero, q)], axis=0)
        sc = lax.dot_general(qh, k_ref[window(s), :], _NT, preferred_element_type=F32)
        sc = sc + tab_ref[type_ref[s]]
        p = jnp.exp2(sc - jnp.max(sc, axis=-1, keepdims=True))
        p_ref[slot] = p.astype(BF16)
        r_ref[slot] = jnp.broadcast_to(1.0 / jnp.sum(p, axis=-1, keepdims=True), (2 * nq, LANES))

    def attend(s, slot):
        o = jnp.dot(p_ref[slot], v_ref[window(s), :], preferred_element_type=F32) * r_ref[slot]
        o_ref[rows(s), :] = jnp.where(lo, o[:nq], o[nq:]).astype(BF16)

    per_iter = min(NA_STEPS_PER_ITER, steps)
    softmax(0, 0)

    def run(s0, count, last):
        for j in range(count):
            if not (last and j == count - 1):
                softmax(s0 + j + 1, (j + 1) % 2)
            attend(s0 + j, j % 2)

    def body(i, carry):
        run(i * per_iter, per_iter, False)
        return carry

    lax.fori_loop(0, steps // per_iter - 1, body, 0)
    run(steps - per_iter, per_iter, True)


def _neighbourhood(proj, tables, batch, seq):
    m, _ = proj.shape
    rows = seq // GRID_W
    starts, types, _, win = _na_step_geometry(rows)
    pairs = NA_HEADS // 2
    tok = NA_HEADS * NA_HEAD_DIM
    n_types, nq2, nk = tables.shape[1:]
    hblk = lambda off: pl.BlockSpec((seq, LANES), lambda b, p: (b, off + p))
    smem = pl.BlockSpec(memory_space=pltpu.SMEM)
    return pl.pallas_call(
        functools.partial(_na_kernel, win=win),
        grid=(batch, pairs),
        in_specs=[smem, smem, hblk(0), hblk(pairs), hblk(2 * pairs),
                  pl.BlockSpec((None, n_types, nq2, nk), lambda b, p: (p, 0, 0, 0))],
        out_specs=pl.BlockSpec((seq, LANES), lambda b, p: (b, p)),
        out_shape=jax.ShapeDtypeStruct((m, tok), BF16),
        scratch_shapes=[pltpu.VMEM((2, nq2, nk), BF16), pltpu.VMEM((2, nq2, LANES), F32)],
        compiler_params=pltpu.CompilerParams(dimension_semantics=("arbitrary", "arbitrary"),
                                             vmem_limit_bytes=VMEM_LIMIT),
        name="neighbourhood",
    )(jnp.asarray(starts, jnp.int32), jnp.asarray(types, jnp.int32), proj, proj, proj, tables)


def _post_kernel(y_ref, mq_ref, mk_ref, mv_ref, h_ref, wo_ref, g_ref, w1_ref, w2_ref, o_ref, f_ref,
                 h1_ref, a_ref):
    tok = y_ref.shape[1]
    hidden = w2_ref.shape[0]
    half = y_ref.shape[0] // 2
    groups = (slice(0, half), slice(half, 2 * half))

    def head(rows):
        m = _memory_attention(mq_ref[rows, :], mk_ref[...], mv_ref[...])
        y = jnp.dot(y_ref[rows, :], wo_ref[:tok, :], preferred_element_type=F32)
        y += jnp.dot(m.astype(BF16), wo_ref[tok:, :], preferred_element_type=F32)
        h1 = h_ref[rows, :] + _rms(y, g_ref[1:2, :])
        h1_ref[rows, :] = h1
        a_ref[rows, :] = _rms(h1, g_ref[2:3, :]).astype(BF16)

    def ffn_chunk(rows, c):
        cols = slice(c * FFN_CHUNK, (c + 1) * FFN_CHUNK)
        a = a_ref[rows, :]
        gate = jnp.dot(a, w1_ref[:, cols], preferred_element_type=F32)
        up = jnp.dot(a, w1_ref[:, hidden + c * FFN_CHUNK:hidden + (c + 1) * FFN_CHUNK],
                     preferred_element_type=F32)
        act = (gate * jax.nn.sigmoid(gate) * up).astype(BF16)
        part = jnp.dot(act, w2_ref[cols, :], preferred_element_type=F32)
        if c == 0:
            f_ref[rows, :] = part
        else:
            f_ref[rows, :] += part

    def tail(rows):
        o_ref[rows, :] = h1_ref[rows, :] + _rms(f_ref[rows, :], g_ref[3:4, :])

    n_ffn = hidden // FFN_CHUNK
    head(groups[0])
    for c in range(n_ffn):
        ffn_chunk(groups[0], c)
        if c == n_ffn // 2:
            head(groups[1])
    for c in range(n_ffn):
        ffn_chunk(groups[1], c)
        if c == n_ffn // 2:
            tail(groups[0])
    tail(groups[1])


def _post(y, proj, mem_k, mem_v, h, w_out, w_out_layer, gains, w1, w2, layer, tm, seq, mem_len):
    m, d = h.shape
    tok = y.shape[1]
    mq_block = (proj.shape[1] - MEM_WIDTH) // MEM_WIDTH
    tiles_per_seq = seq // tm
    hidden = w2.shape[1]
    assert hidden % FFN_CHUNK == 0 and w1.shape[1:] == (d, 2 * hidden)
    return pl.pallas_call(
        _post_kernel,
        grid=(m // tm,),
        in_specs=[pl.BlockSpec((tm, tok), lambda i: (i, 0)),
                  pl.BlockSpec((tm, MEM_WIDTH), lambda i: (i, mq_block)),
                  pl.BlockSpec((mem_len, MEM_WIDTH), lambda i: (i // tiles_per_seq, 0)),
                  pl.BlockSpec((mem_len, MEM_WIDTH), lambda i: (i // tiles_per_seq, 0)),
                  pl.BlockSpec((tm, d), lambda i: (i, 0)),
                  _layer_spec(w_out, w_out_layer),
                  _layer_spec(gains, layer),
                  _layer_spec(w1, layer),
                  _layer_spec(w2, layer)],
        out_specs=pl.BlockSpec((tm, d), lambda i: (i, 0)),
        out_shape=jax.ShapeDtypeStruct((m, d), F32),
        scratch_shapes=[pltpu.VMEM((tm, d), F32), pltpu.VMEM((tm, d), F32), pltpu.VMEM((tm, d), BF16)],
        compiler_params=pltpu.CompilerParams(dimension_semantics=("arbitrary",),
                                             vmem_limit_bytes=VMEM_LIMIT),
        name="post",
    )(y, proj, mem_k, mem_v, h, w_out, gains, w1, w2)


def kernel(x, mem, norm_g, mem_norm_g, mem_w_kv, ret_w_in, ret_w_out, ret_gn_g, ret_gn_b, ret_decay,
           na_w_in, na_w_out, na_rpb, ffn_w_in, ffn_w_out):
    batch, seq, d = x.shape
    mem_len = mem.shape[1]
    depth = norm_g.shape[0]
    hidden = ffn_w_out.shape[1]
    tm_in = min(IN_PROJ_ROWS, seq)
    tm_post = min(POST_ROWS, seq)
    rows = seq // GRID_W

    mem_k, mem_v = _mem_kv(mem.reshape(batch * mem_len, d), mem_norm_g, mem_w_kv.astype(BF16), mem_len)
    rot = _rotary_tables(seq)
    gains = norm_g.astype(F32)
    ret_w_in, ret_w_out, na_w_in, na_w_out, ffn_w_in, ffn_w_out = (
        w.astype(BF16) for w in (ret_w_in, ret_w_out, na_w_in, na_w_out, ffn_w_in, ffn_w_out))
    h = x.reshape(batch * seq, d)
    for i in range(depth):
        j = i // 2
        if i % 2 == 0:
            proj = _in_proj(h, gains, i, ret_w_in, j, rot, tm_in, seq)
            log_gamma = -jnp.exp(ret_decay[j].astype(F32))
            y = _retention(proj, log_gamma, ret_gn_g[j], ret_gn_b[j], batch, seq)
            w_out = ret_w_out
        else:
            proj = _in_proj(h, gains, i, na_w_in, j, None, tm_in, seq)
            y = _neighbourhood(proj, _na_bias_tables(na_rpb[j], rows), batch, seq)
            w_out = na_w_out
        h = _post(y, proj, mem_k, mem_v, h, w_out, j, gains, ffn_w_in, ffn_w_out, i, tm_post, seq,
                  mem_len)
    return h.reshape(batch, seq, d)
```

```python
import functools

import numpy as np
import jax
import jax.numpy as jnp
from jax import lax
from jax.experimental import pallas as pl
from jax.experimental.pallas import tpu as pltpu

F32 = jnp.float32
BF16 = jnp.bfloat16

LANES = 128
MEM_HEADS = 4
MEM_HEAD_DIM = 64
MEM_WIDTH = MEM_HEADS * MEM_HEAD_DIM
RET_HEADS = 6
RET_HEAD_DIM = 128
RET_CHUNK = 128
RET_ROPE_BASE = 10000.0
RET_UNROLL = 16
GN_EPS = 1e-5
NA_HEADS = 12
NA_HEAD_DIM = 64
GRID_W = 64
NA_WIN_R = 8
NA_WIN_C = 16
NA_ROWS_PER_STEP = 2
NA_STEPS_PER_ITER = 16
FFN_CHUNK = 256
IN_PROJ_ROWS = 1024
POST_ROWS = 1024
EPS = 1e-6
NEG = -1e30
LOG2E = 1.4426950408889634
VMEM_LIMIT = 56 * 1024 * 1024

_NT = (((1,), (1,)), ((), ()))


def _rms(x, g):
    return x * lax.rsqrt(jnp.mean(x * x, axis=-1, keepdims=True) + EPS) * g


def _const_spec(shape):
    nd = len(shape)
    return pl.BlockSpec(shape, lambda *_: (0,) * nd, pipeline_mode=pl.Buffered(1))


def _layer_spec(stacked, layer):
    shape = stacked.shape[1:]
    nd = len(shape)
    return pl.BlockSpec((None,) + shape, lambda *_: (layer,) + (0,) * nd, pipeline_mode=pl.Buffered(1))


def _mem_kv_kernel(mem_ref, g_ref, w_ref, k_ref, v_ref):
    a = _rms(mem_ref[...], g_ref[...]).astype(BF16)
    kv = jnp.dot(a, w_ref[...], preferred_element_type=F32)
    k_ref[...] = kv[:, :MEM_WIDTH].astype(BF16)
    v_ref[...] = kv[:, MEM_WIDTH:].astype(BF16)


def _mem_kv(mem2d, g, w_kv, mem_len):
    m, d = mem2d.shape
    return pl.pallas_call(
        _mem_kv_kernel,
        grid=(m // mem_len,),
        in_specs=[pl.BlockSpec((mem_len, d), lambda i: (i, 0)),
                  _const_spec((1, d)),
                  _const_spec((d, 2 * MEM_WIDTH))],
        out_specs=[pl.BlockSpec((mem_len, MEM_WIDTH), lambda i: (i, 0))] * 2,
        out_shape=[jax.ShapeDtypeStruct((m, MEM_WIDTH), BF16)] * 2,
        name="mem_kv",
    )(mem2d, g.reshape(1, d), w_kv)


def _memory_attention(mq, mk, mv):
    lane = lax.broadcasted_iota(jnp.int32, (1, MEM_WIDTH), 1)
    mq = mq * (MEM_HEAD_DIM ** -0.5)
    m = jnp.zeros(mq.shape, F32)
    for hd in range(MEM_HEADS):
        sel = (lane >= hd * MEM_HEAD_DIM) & (lane < (hd + 1) * MEM_HEAD_DIM)
        qh = jnp.where(sel, mq, jnp.zeros_like(mq))
        sc = lax.dot_general(qh, mk, _NT, preferred_element_type=F32)
        sc = sc - jnp.max(sc, axis=-1, keepdims=True)
        p = jnp.exp(sc)
        l = jnp.sum(p, axis=-1, keepdims=True)
        vh = jnp.where(sel, mv, jnp.zeros_like(mv))
        m += jnp.dot(p.astype(BF16), vh, preferred_element_type=F32) / l
    return m


def _in_proj_kernel(h_ref, g_ref, w_ref, *rest, n_rot):
    if n_rot:
        cos_ref, sin_ref, o_ref = rest
    else:
        (o_ref,) = rest
    a = _rms(h_ref[...], g_ref[0:1, :]).astype(BF16)
    n = o_ref.shape[1]
    step = 2 * LANES
    for c in range(n // step):
        y = jnp.dot(a, w_ref[:, c * step:(c + 1) * step], preferred_element_type=F32)
        if 2 * c < n_rot:
            halves = []
            for y1 in (y[:, :LANES], y[:, LANES:]):
                halves.append(y1 * cos_ref[...] + pltpu.roll(y1, LANES // 2, 1) * sin_ref[...])
            y = jnp.concatenate(halves, axis=1)
        o_ref[:, c * step:(c + 1) * step] = y.astype(BF16)


def _in_proj(h, gains, layer, w, w_layer, rot, tm, seq):
    m, d = h.shape
    n = w.shape[2]
    in_specs = [pl.BlockSpec((tm, d), lambda i: (i, 0)), _layer_spec(gains, layer), _layer_spec(w, w_layer)]
    args = [h, gains, w]
    n_rot = 0
    if rot is not None:
        n_rot = 2 * RET_HEADS
        tiles_per_seq = seq // tm
        in_specs += [pl.BlockSpec((tm, LANES), lambda i: (i % tiles_per_seq, 0))] * 2
        args += list(rot)
    return pl.pallas_call(
        functools.partial(_in_proj_kernel, n_rot=n_rot),
        grid=(m // tm,),
        in_specs=in_specs,
        out_specs=pl.BlockSpec((tm, n), lambda i: (i, 0)),
        out_shape=jax.ShapeDtypeStruct((m, n), BF16),
        compiler_params=pltpu.CompilerParams(dimension_semantics=("arbitrary",),
                                             vmem_limit_bytes=VMEM_LIMIT),
        name="in_proj",
    )(*args)


def _rotary_tables(seq):
    half = RET_HEAD_DIM // 2
    inv = RET_ROPE_BASE ** (-jnp.arange(half, dtype=F32) / half)
    ang = jnp.arange(seq, dtype=F32)[:, None] * inv[None, :]
    c, s = jnp.cos(ang), jnp.sin(ang)
    return jnp.concatenate([c, c], axis=1), jnp.concatenate([-s, s], axis=1)


def _ret_kernel(lg_ref, q_ref, k_ref, v_ref, g_ref, gng_ref, gnb_ref, o_ref, p_ref, y_ref, kv_ref, st_ref):
    hd = pl.program_id(1)
    c = RET_CHUNK
    dk = RET_HEAD_DIM
    n_chunks = q_ref.shape[0] // c
    lg_f = lg_ref[0, hd]
    lg_b = lg_ref[1, hd]
    scale = RET_HEAD_DIM ** -0.5

    row = lax.broadcasted_iota(jnp.int32, (c, c), 0)
    col = lax.broadcasted_iota(jnp.int32, (c, c), 1)
    rel = (row - col).astype(F32)
    decay = (jnp.where(rel >= 0, jnp.exp(lg_f * jnp.maximum(rel, 0.0)), 0.0)
             + jnp.where(rel <= 0, jnp.exp(lg_b * jnp.maximum(-rel, 0.0)), 0.0)) * scale
    tok_row = row.astype(F32)
    tok_col = col.astype(F32)
    cd_f = jnp.exp(jnp.full((1, 1), c, F32) * lg_f)
    cd_b = jnp.exp(jnp.full((1, 1), c, F32) * lg_b)

    def chunk(n):
        return pl.ds(pl.multiple_of(n * c, c), c)

    kdec_f = jnp.exp(lg_f * (c - 1 - tok_col))
    kdec_b = jnp.exp(lg_b * tok_col)

    group = min(RET_UNROLL, n_chunks)
    n_groups = n_chunks // group

    def kv_chunk(n):
        kt = k_ref[chunk(n), :].astype(F32).T
        kd = jnp.concatenate([(kt * kdec_f).astype(BF16), (kt * kdec_b).astype(BF16)], axis=0)
        kv_ref[n] = jnp.dot(kd, v_ref[chunk(n), :], preferred_element_type=F32)
        s = jnp.dot(q_ref[chunk(n), :], kt.astype(BF16), preferred_element_type=F32) * decay
        p_ref[chunk(n), :] = s.astype(BF16)

    def fwd_step(n, s_f):
        st_ref[n, :dk, :] = s_f.astype(BF16)
        return s_f * cd_f + kv_ref[n, :dk, :]

    def bwd_step(n, s_b):
        st_ref[n, dk:, :] = s_b.astype(BF16)
        return s_b * cd_b + kv_ref[n, dk:, :]

    for j in range(group):
        kv_chunk(j)

    def kv_body(g, s_f):
        for j in range(group):
            s_f = fwd_step(g * group + j, s_f)
            kv_chunk((g + 1) * group + j)
        return s_f

    zero = jnp.zeros((dk, RET_HEAD_DIM), F32)
    s_f = lax.fori_loop(0, n_groups - 1, kv_body, zero)
    for j in range(group):
        s_f = fwd_step((n_groups - 1) * group + j, s_f)
    lax.fori_loop(0, n_chunks, lambda i, s_b: bwd_step(n_chunks - 1 - i, s_b), zero, unroll=RET_UNROLL)

    qdec_f = jnp.exp(lg_f * (tok_row + 1)) * scale
    qdec_b = jnp.exp(lg_b * (c - tok_row)) * scale

    def y_chunk(n):
        qf32 = q_ref[chunk(n), :].astype(F32)
        qd = jnp.concatenate([(qf32 * qdec_f).astype(BF16), (qf32 * qdec_b).astype(BF16)], axis=1)
        y_ref[chunk(n), :] = (jnp.dot(p_ref[chunk(n), :], v_ref[chunk(n), :], preferred_element_type=F32)
                              + jnp.dot(qd, st_ref[n], preferred_element_type=F32))

    def norm_chunk(n):
        y = y_ref[chunk(n), :]
        mu = jnp.mean(y, axis=-1, keepdims=True)
        var = jnp.maximum(jnp.mean(y * y, axis=-1, keepdims=True) - mu * mu, 0.0)
        yn = (y - mu) * lax.rsqrt(var + GN_EPS) * gng_ref[...] + gnb_ref[...]
        gate = g_ref[chunk(n), :].astype(F32)
        o_ref[chunk(n), :] = (gate * jax.nn.sigmoid(gate) * yn).astype(BF16)

    for j in range(group):
        y_chunk(j)

    def pipe_body(g, carry):
        for j in range(group):
            norm_chunk(g * group + j)
            y_chunk((g + 1) * group + j)
        return carry

    lax.fori_loop(0, n_groups - 1, pipe_body, 0)
    for j in range(group):
        norm_chunk((n_groups - 1) * group + j)


def _retention(proj, log_gamma, gn_g, gn_b, batch, seq):
    m, _ = proj.shape
    n_chunks = seq // RET_CHUNK
    hblk = lambda off: pl.BlockSpec((seq, RET_HEAD_DIM), lambda b, h: (b, off + h))
    vec = pl.BlockSpec((1, RET_HEAD_DIM), lambda b, h: (0, h))
    tok = RET_HEADS * RET_HEAD_DIM
    return pl.pallas_call(
        _ret_kernel,
        grid=(batch, RET_HEADS),
        in_specs=[pl.BlockSpec(memory_space=pltpu.SMEM),
                  hblk(0), hblk(RET_HEADS), hblk(2 * RET_HEADS), hblk(3 * RET_HEADS), vec, vec],
        out_specs=pl.BlockSpec((seq, RET_HEAD_DIM), lambda b, h: (b, h)),
        out_shape=jax.ShapeDtypeStruct((m, tok), BF16),
        scratch_shapes=[pltpu.VMEM((seq, RET_CHUNK), BF16),
                        pltpu.VMEM((seq, RET_HEAD_DIM), F32),
                        pltpu.VMEM((n_chunks, 2 * RET_HEAD_DIM, RET_HEAD_DIM), F32),
                        pltpu.VMEM((n_chunks, 2 * RET_HEAD_DIM, RET_HEAD_DIM), BF16)],
        compiler_params=pltpu.CompilerParams(dimension_semantics=("arbitrary", "arbitrary"),
                                             vmem_limit_bytes=VMEM_LIMIT),
        name="retention",
    )(log_gamma, proj, proj, proj, proj, gn_g.reshape(1, tok), gn_b.reshape(1, tok))


def _na_step_geometry(rows):
    rg = NA_ROWS_PER_STEP
    wr = min(NA_WIN_R, rows)
    win = wr + rg - 1
    steps = rows // rg
    starts = [int(np.clip(np.clip(s * rg - wr // 2, 0, rows - wr), 0, rows - win)) for s in range(steps)]
    reps, types = [], []
    for s in range(steps):
        sig = (starts[s] - s * rg,
               tuple(int(np.clip(s * rg + j - wr // 2, 0, rows - wr)) - s * rg for j in range(rg)))
        for t, (sig_t, _) in enumerate(reps):
            if sig_t == sig:
                types.append(t)
                break
        else:
            types.append(len(reps))
            reps.append((sig, s))
    return starts, types, [s for _, s in reps], win


def _na_bias_tables(rpb, rows):
    rg = NA_ROWS_PER_STEP
    wr = min(NA_WIN_R, rows)
    starts, _, rep_steps, win = _na_step_geometry(rows)
    heads, n_dr, n_dc = rpb.shape
    assert 2 * GRID_W == LANES and n_dc <= GRID_W
    pad = rg + wr
    lo = np.zeros((len(rep_steps), rg), np.int32)
    row_ok = np.zeros((len(rep_steps), rg * win), np.int32)
    for t, s in enumerate(rep_steps):
        for j in range(rg):
            qrow = s * rg + j
            lo[t, j] = starts[s] - qrow + NA_WIN_R - 1 + pad
            rs = int(np.clip(qrow - wr // 2, 0, rows - wr))
            krow = starts[s] + np.arange(win)
            row_ok[t, j * win:(j + 1) * win] = (krow >= rs) & (krow < rs + wr)
    rpb_rows = jnp.pad(rpb.astype(F32), ((0, 0), (pad, pad), (0, LANES - n_dc)))
    smem = pl.BlockSpec(memory_space=pltpu.SMEM)
    return pl.pallas_call(
        functools.partial(_na_table_kernel, win=win),
        grid=(heads // 2, len(rep_steps)),
        in_specs=[smem, smem, pl.BlockSpec((2, n_dr + 2 * pad, LANES), lambda p, t: (p, 0, 0))],
        out_specs=pl.BlockSpec((None, None, 2 * rg * GRID_W, win * GRID_W), lambda p, t: (p, t, 0, 0)),
        out_shape=jax.ShapeDtypeStruct((heads // 2, len(rep_steps), 2 * rg * GRID_W, win * GRID_W), F32),
        name="na_tables",
    )(jnp.asarray(lo), jnp.asarray(row_ok), rpb_rows)


def _na_table_kernel(lo_ref, ok_ref, r_ref, o_ref, *, win):
    t = pl.program_id(1)
    rg = NA_ROWS_PER_STEP
    shape = (GRID_W, LANES)
    qc = lax.broadcasted_iota(jnp.int32, shape, 0)
    lane = lax.broadcasted_iota(jnp.int32, shape, 1)
    kc = lane & (GRID_W - 1)
    cs = jnp.clip(qc - NA_WIN_C // 2, 0, GRID_W - NA_WIN_C)
    col_ok = (kc >= cs) & (kc < cs + NA_WIN_C)
    for hd in range(2):
        for j in range(rg):
            base = lo_ref[t, j]
            rows = slice((hd * rg + j) * GRID_W, (hd * rg + j + 1) * GRID_W)
            for w0 in range(0, win, 2):
                pair = w0 + 1 < win
                x = r_ref[hd, pl.ds(base + w0, 1), :]
                ok_lo = ok_ref[t, j * win + w0]
                ok_hi = ok_ref[t, j * win + w0 + 1] if pair else ok_lo
                if pair:
                    x = x + pltpu.roll(r_ref[hd, pl.ds(base + w0 + 1, 1), :], GRID_W, 1)
                row_ok = jnp.where(lane < GRID_W, ok_lo, ok_hi) != 0
                band = pltpu.roll(jnp.broadcast_to(x, shape), LANES - (NA_WIN_C - 1), 1,
                                  stride=1, stride_axis=0)
                val = jnp.where(col_ok & row_ok, band * LOG2E, NEG)
                if pair:
                    o_ref[rows, w0 * GRID_W:(w0 + 2) * GRID_W] = val
                else:
                    o_ref[rows, w0 * GRID_W:(w0 + 1) * GRID_W] = val[:, :GRID_W]


def _na_kernel(start_ref, type_ref, q_ref, k_ref, v_ref, tab_ref, o_ref, p_ref, r_ref, *, win):
    nq = NA_ROWS_PER_STEP * GRID_W
    nk = win * GRID_W
    steps = q_ref.shape[0] // nq
    lo = lax.broadcasted_iota(jnp.int32, (1, LANES), 1) < NA_HEAD_DIM
    scale = NA_HEAD_DIM ** -0.5 * LOG2E

    def rows(s):
        return pl.ds(pl.multiple_of(s * nq, nq), nq)

    def window(s):
        return pl.ds(pl.multiple_of(start_ref[s] * GRID_W, GRID_W), nk)

    def softmax(s, slot):
        q = (q_ref[rows(s), :].astype(F32) * scale).astype(BF16)
        zero = jnp.zeros_like(q)
        qh = jnp.concatenate([jnp.where(lo, q, zero), jnp.where(lo, zero, q)], axis=0)
        sc = lax.dot_general(qh, k_ref[window(s), :], _NT, preferred_element_type=F32)
        sc = sc + tab_ref[type_ref[s]]
        p = jnp.exp2(sc - jnp.max(sc, axis=-1, keepdims=True))
        p_ref[slot] = p.astype(BF16)
        r_ref[slot] = jnp.broadcast_to(1.0 / jnp.sum(p, axis=-1, keepdims=True), (2 * nq, LANES))

    def attend(s, slot):
        o = jnp.dot(p_ref[slot], v_ref[window(s), :], preferred_element_type=F32) * r_ref[slot]
        o_ref[rows(s), :] = jnp.where(lo, o[:nq], o[nq:]).astype(BF16)

    per_iter = min(NA_STEPS_PER_ITER, steps)
    softmax(0, 0)

    def run(s0, count, last):
        for j in range(count):
            if not (last and j == count - 1):
                softmax(s0 + j + 1, (j + 1) % 2)
            attend(s0 + j, j % 2)

    def body(i, carry):
        run(i * per_iter, per_iter, False)
        return carry

    lax.fori_loop(0, steps // per_iter - 1, body, 0)
    run(steps - per_iter, per_iter, True)


def _neighbourhood(proj, tables, batch, seq):
    m, _ = proj.shape
    rows = seq // GRID_W
    starts, types, _, win = _na_step_geometry(rows)
    pairs = NA_HEADS // 2
    tok = NA_HEADS * NA_HEAD_DIM
    n_types, nq2, nk = tables.shape[1:]
    hblk = lambda off: pl.BlockSpec((seq, LANES), lambda b, p: (b, off + p))
    smem = pl.BlockSpec(memory_space=pltpu.SMEM)
    return pl.pallas_call(
        functools.partial(_na_kernel, win=win),
        grid=(batch, pairs),
        in_specs=[smem, smem, hblk(0), hblk(pairs), hblk(2 * pairs),
                  pl.BlockSpec((None, n_types, nq2, nk), lambda b, p: (p, 0, 0, 0))],
        out_specs=pl.BlockSpec((seq, LANES), lambda b, p: (b, p)),
        out_shape=jax.ShapeDtypeStruct((m, tok), BF16),
        scratch_shapes=[pltpu.VMEM((2, nq2, nk), BF16), pltpu.VMEM((2, nq2, LANES), F32)],
        compiler_params=pltpu.CompilerParams(dimension_semantics=("arbitrary", "arbitrary"),
                                             vmem_limit_bytes=VMEM_LIMIT),
        name="neighbourhood",
    )(jnp.asarray(starts, jnp.int32), jnp.asarray(types, jnp.int32), proj, proj, proj, tables)


def _post_kernel(y_ref, mq_ref, mk_ref, mv_ref, h_ref, wo_ref, g_ref, w1_ref, w2_ref, o_ref, f_ref,
                 h1_ref, a_ref, act_ref):
    tok = y_ref.shape[1]
    hidden = w2_ref.shape[0]
    half = y_ref.shape[0] // 2
    groups = (slice(0, half), slice(half, 2 * half))

    def head(rows):
        m = _memory_attention(mq_ref[rows, :], mk_ref[...], mv_ref[...])
        y = jnp.dot(y_ref[rows, :], wo_ref[:tok, :], preferred_element_type=F32)
        y += jnp.dot(m.astype(BF16), wo_ref[tok:, :], preferred_element_type=F32)
        h1 = h_ref[rows, :] + _rms(y, g_ref[1:2, :])
        h1_ref[rows, :] = h1
        a_ref[rows, :] = _rms(h1, g_ref[2:3, :]).astype(BF16)

    def ffn_chunk(rows, c):
        cols = slice(c * FFN_CHUNK, (c + 1) * FFN_CHUNK)
        a = a_ref[rows, :]
        gate = jnp.dot(a, w1_ref[:, cols], preferred_element_type=F32)
        up = jnp.dot(a, w1_ref[:, hidden + c * FFN_CHUNK:hidden + (c + 1) * FFN_CHUNK],
                     preferred_element_type=F32)
        act_ref[rows, cols] = (gate * jax.nn.sigmoid(gate) * up).astype(BF16)

    def down(rows):
        f_ref[rows, :] = jnp.dot(act_ref[rows, :], w2_ref[...], preferred_element_type=F32)

    def tail(rows):
        o_ref[rows, :] = h1_ref[rows, :] + _rms(f_ref[rows, :], g_ref[3:4, :])

    n_ffn = hidden // FFN_CHUNK
    head(groups[0])
    for c in range(n_ffn):
        ffn_chunk(groups[0], c)
        if c == n_ffn // 2:
            head(groups[1])
    down(groups[0])
    for c in range(n_ffn):
        ffn_chunk(groups[1], c)
        if c == n_ffn // 2:
            tail(groups[0])
    down(groups[1])
    tail(groups[1])


def _post(y, proj, mem_k, mem_v, h, w_out, w_out_layer, gains, w1, w2, layer, tm, seq, mem_len):
    m, d = h.shape
    tok = y.shape[1]
    mq_block = (proj.shape[1] - MEM_WIDTH) // MEM_WIDTH
    tiles_per_seq = seq // tm
    hidden = w2.shape[1]
    assert hidden % FFN_CHUNK == 0 and w1.shape[1:] == (d, 2 * hidden)
    return pl.pallas_call(
        _post_kernel,
        grid=(m // tm,),
        in_specs=[pl.BlockSpec((tm, tok), lambda i: (i, 0)),
                  pl.BlockSpec((tm, MEM_WIDTH), lambda i: (i, mq_block)),
                  pl.BlockSpec((mem_len, MEM_WIDTH), lambda i: (i // tiles_per_seq, 0)),
                  pl.BlockSpec((mem_len, MEM_WIDTH), lambda i: (i // tiles_per_seq, 0)),
                  pl.BlockSpec((tm, d), lambda i: (i, 0)),
                  _layer_spec(w_out, w_out_layer),
                  _layer_spec(gains, layer),
                  _layer_spec(w1, layer),
                  _layer_spec(w2, layer)],
        out_specs=pl.BlockSpec((tm, d), lambda i: (i, 0)),
        out_shape=jax.ShapeDtypeStruct((m, d), F32),
        scratch_shapes=[pltpu.VMEM((tm, d), F32), pltpu.VMEM((tm, d), F32), pltpu.VMEM((tm, d), BF16),
                        pltpu.VMEM((tm, hidden), BF16)],
        compiler_params=pltpu.CompilerParams(dimension_semantics=("arbitrary",),
                                             vmem_limit_bytes=VMEM_LIMIT),
        name="post",
    )(y, proj, mem_k, mem_v, h, w_out, gains, w1, w2)


def kernel(x, mem, norm_g, mem_norm_g, mem_w_kv, ret_w_in, ret_w_out, ret_gn_g, ret_gn_b, ret_decay,
           na_w_in, na_w_out, na_rpb, ffn_w_in, ffn_w_out):
    batch, seq, d = x.shape
    mem_len = mem.shape[1]
    depth = norm_g.shape[0]
    hidden = ffn_w_out.shape[1]
    tm_in = min(IN_PROJ_ROWS, seq)
    tm_post = min(POST_ROWS, seq)
    rows = seq // GRID_W

    mem_k, mem_v = _mem_kv(mem.reshape(batch * mem_len, d), mem_norm_g, mem_w_kv.astype(BF16), mem_len)
    rot = _rotary_tables(seq)
    gains = norm_g.astype(F32)
    ret_w_in, ret_w_out, na_w_in, na_w_out, ffn_w_in, ffn_w_out = (
        w.astype(BF16) for w in (ret_w_in, ret_w_out, na_w_in, na_w_out, ffn_w_in, ffn_w_out))
    h = x.reshape(batch * seq, d)
    for i in range(depth):
        j = i // 2
        if i % 2 == 0:
            proj = _in_proj(h, gains, i, ret_w_in, j, rot, tm_in, seq)
            log_gamma = -jnp.exp(ret_decay[j].astype(F32))
            y = _retention(proj, log_gamma, ret_gn_g[j], ret_gn_b[j], batch, seq)
            w_out = ret_w_out
        else:
            proj = _in_proj(h, gains, i, na_w_in, j, None, tm_in, seq)
            y = _neighbourhood(proj, _na_bias_tables(na_rpb[j], rows), batch, seq)
            w_out = na_w_out
        h = _post(y, proj, mem_k, mem_v, h, w_out, j, gains, ffn_w_in, ffn_w_out, i, tm_post, seq,
                  mem_len)
    return h.reshape(batch, seq, d)
```
